```python
import jax
import jax.numpy as jnp
from jax import lax
import numpy as np

D_MODEL = 2048
BATCH = 1
SEQ = 8192
DEPTH = 2
DEC_BATCH = 128
DEC_SEQ = 8
PAST_LEN = 16384
PAGE_SIZE = 128

N_A = DEPTH // 2
N_B = DEPTH - N_A
POOL_WINDOWS = (2, 4, 8, 16)
N_POOL_GROUPS = len(POOL_WINDOWS)
POOL_GROUP = D_MODEL // N_POOL_GROUPS
POOL_BUF = max(POOL_WINDOWS) - 1
N_HEADS = 16
Q_LORA = 512
KV_LORA = 512
D_NOPE = 128
D_ROPE = 64
D_QK = D_NOPE + D_ROPE
D_V = 128
ROPE_BASE = 10000.0
ATTN_SCALE = D_QK ** -0.5
Q_BLOCK = 128
N_EXPERTS = 32
TOP_K = 4
D_EXPERT = D_MODEL
SWIGLU_ALPHA = 1.702
SWIGLU_LIMIT = 7.0
MOE_BLOCK = 128
EPS = 1e-6

kernel_name = 'yoco_pool_mla_moe_step'


def rmsnorm(x, g):
    xf = x.astype(jnp.float32)
    y = xf * lax.rsqrt(jnp.mean(xf * xf, axis=-1, keepdims=True) + EPS)
    return (y * g.astype(jnp.float32)).astype(x.dtype)


def rope(x, pos):
    half = D_ROPE // 2
    inv_freq = jnp.power(ROPE_BASE, -jnp.arange(half, dtype=jnp.float32) / half)
    ang = pos.astype(jnp.float32)[:, None] * inv_freq[None, :]
    shape = (ang.shape[0],) + (1,) * (x.ndim - 3) + (half,)
    cos = jnp.cos(ang).reshape(shape)
    sin = jnp.sin(ang).reshape(shape)
    x1 = x[..., :half].astype(jnp.float32)
    x2 = x[..., half:].astype(jnp.float32)
    return jnp.concatenate([x1 * cos - x2 * sin, x2 * cos + x1 * sin], axis=-1).astype(x.dtype)


def pool_mix(xn, buf, pos, w_pool, s_pool):
    S = xn.shape[1]
    cat = jnp.concatenate([buf.astype(xn.dtype), xn], axis=1)
    cs = jnp.pad(jnp.cumsum(cat.astype(jnp.float32), axis=1), ((0, 0), (1, 0), (0, 0)))
    end = POOL_BUF + 1
    outs = []
    for g, w in enumerate(POOL_WINDOWS):
        ch = slice(g * POOL_GROUP, (g + 1) * POOL_GROUP)
        win = cs[:, end:end + S, ch] - cs[:, end - w:end - w + S, ch]
        cnt = jnp.minimum(pos + 1, w).astype(jnp.float32)[None, :, None]
        outs.append(win / cnt - xn[:, :, ch].astype(jnp.float32))
    d = jnp.stack(outs, axis=2).astype(xn.dtype)
    y = jnp.einsum('bsgc,gce->bsge', d, w_pool).reshape(xn.shape) * s_pool
    return y, cat[:, -POOL_BUF:]


def moe(xn, w_router, b_router, w_up, b_up, w_down, b_down):
    B, S, D = xn.shape
    T = B * S
    xt = xn.reshape(T, D)
    logits = (xt @ w_router + b_router).astype(jnp.float32)
    top_val, top_idx = lax.top_k(logits, TOP_K)
    gates = jax.nn.softmax(top_val, axis=-1)
    flat_e = top_idx.reshape(-1).astype(jnp.int32)
    flat_tok = jnp.arange(T * TOP_K, dtype=jnp.int32) // TOP_K
    flat_g = gates.reshape(-1)
    order = jnp.argsort(flat_e)
    se = flat_e[order]
    counts = jnp.bincount(flat_e, length=N_EXPERTS).astype(jnp.int32)
    start = jnp.cumsum(counts) - counts
    padded = (counts + MOE_BLOCK - 1) // MOE_BLOCK * MOE_BLOCK
    pend = jnp.cumsum(padded)
    pstart = pend - padded
    dest = pstart[se] + jnp.arange(T * TOP_K, dtype=jnp.int32) - start[se]
    n_blocks = -(-(T * TOP_K) // MOE_BLOCK) + N_EXPERTS
    n_rows = n_blocks * MOE_BLOCK
    row_tok = jnp.full((n_rows,), T, jnp.int32).at[dest].set(flat_tok[order])
    row_gate = jnp.zeros((n_rows,), jnp.float32).at[dest].set(flat_g[order])
    block_e = jnp.minimum(jnp.searchsorted(pend, jnp.arange(n_blocks, dtype=jnp.int32) * MOE_BLOCK, side='right'), N_EXPERTS - 1)
    x_pad = jnp.concatenate([xt, jnp.zeros((1, D), xt.dtype)], axis=0)

    def block_fn(args):
        tok, gate, e = args
        h = x_pad[tok] @ w_up[e] + b_up[e]
        hg = jnp.minimum(h[:, :D_EXPERT], SWIGLU_LIMIT)
        hl = jnp.clip(h[:, D_EXPERT:], -SWIGLU_LIMIT, SWIGLU_LIMIT)
        a = hg * jax.nn.sigmoid(SWIGLU_ALPHA * hg) * (hl + 1.0)
        y = a @ w_down[e] + b_down[e]
        return y * gate[:, None].astype(y.dtype)

    ys = lax.map(block_fn, (row_tok.reshape(n_blocks, MOE_BLOCK), row_gate.reshape(n_blocks, MOE_BLOCK), block_e))
    out = jnp.zeros((T + 1, D), ys.dtype).at[row_tok].add(ys.reshape(n_rows, D))[:T]
    return out.reshape(B, S, D).astype(xn.dtype)


def shared_latent(x, pos, g_kv, w_dkv, g_ckv):
    kv = rmsnorm(x, g_kv) @ w_dkv
    ckv = rmsnorm(kv[..., :KV_LORA], g_ckv)
    kpe = rope(kv[..., KV_LORA:], pos)
    return ckv, kpe


def mla_query(xn, pos, w_dq, g_q, w_uq):
    cq = rmsnorm(xn @ w_dq, g_q)
    q = jnp.einsum('bsq,qhd->bshd', cq, w_uq)
    return q[..., :D_NOPE], rope(q[..., D_NOPE:], pos)


def attend_prompt(q_nope, q_pe, k_nope, kpe, v, pos):
    B, S, H, _ = q_nope.shape
    nb = S // Q_BLOCK
    qn = q_nope.reshape(B, nb, Q_BLOCK, H, D_NOPE).transpose(1, 0, 2, 3, 4)
    qp = q_pe.reshape(B, nb, Q_BLOCK, H, D_ROPE).transpose(1, 0, 2, 3, 4)
    qpos = pos.reshape(nb, Q_BLOCK)

    def blk(args):
        qn_b, qp_b, qpos_b = args
        s = (jnp.einsum('bqhd,bkhd->bhqk', qn_b, k_nope) + jnp.einsum('bqhe,bke->bhqk', qp_b, kpe)).astype(jnp.float32) * ATTN_SCALE
        s = jnp.where(pos[None, None, None, :] <= qpos_b[None, None, :, None], s, -jnp.inf)
        p = jax.nn.softmax(s, axis=-1).astype(v.dtype)
        return jnp.einsum('bhqk,bkhd->bqhd', p, v)

    o = lax.map(blk, (qn, qp, qpos))
    return o.transpose(1, 0, 2, 3, 4).reshape(B, S, H, D_V)


def attend_sample(q_nope, q_pe, ckv_new, kpe_new, cache_ckv, cache_kpe, page_table, w_uk, w_uv):
    S = q_nope.shape[1]
    q_lat = jnp.einsum('bshd,rhd->bshr', q_nope, w_uk)
    s = (jnp.einsum('bshr,btr->bhst', q_lat, ckv_new) + jnp.einsum('bshe,bte->bhst', q_pe, kpe_new)).astype(jnp.float32) * ATTN_SCALE
    causal = jnp.arange(S)[:, None] >= jnp.arange(S)[None, :]
    s = jnp.where(causal, s, -jnp.inf)
    m = jnp.max(s, axis=-1)
    p = jnp.exp(s - m[..., None])
    l = jnp.sum(p, axis=-1)
    acc = jnp.einsum('bhst,btr->bhsr', p, ckv_new.astype(jnp.float32))

    def step(carry, pages):
        m, l, acc = carry
        ck = cache_ckv[pages]
        kp = cache_kpe[pages]
        s = (jnp.einsum('bshr,bkr->bhsk', q_lat, ck) + jnp.einsum('bshe,bke->bhsk', q_pe, kp)).astype(jnp.float32) * ATTN_SCALE
        m_new = jnp.maximum(m, jnp.max(s, axis=-1))
        corr = jnp.exp(m - m_new)
        p = jnp.exp(s - m_new[..., None])
        l = l * corr + jnp.sum(p, axis=-1)
        acc = acc * corr[..., None] + jnp.einsum('bhsk,bkr->bhsr', p, ck.astype(jnp.float32))
        return (m_new, l, acc), None

    (m, l, acc), _ = lax.scan(step, (m, l, acc), page_table.T)
    o_lat = (acc / l[..., None]).astype(q_nope.dtype)
    return jnp.einsum('bhsr,rhd->bshd', o_lat, w_uv)


def _trunk(x, start, pool_state, past, g_mix, g_ffn, w_pool, s_pool, g_kv, w_dkv, g_ckv,
           w_uk, w_uv, w_dq, g_q, w_uq, w_o, w_router, b_router, w_up, b_up, w_down, b_down, g_final):
    B, S, _ = x.shape
    pos = start + jnp.arange(S, dtype=jnp.int32)
    new_pool = []
    ckv = kpe = k_nope = v = None
    for l in range(DEPTH):
        if l == N_A:
            ckv, kpe = shared_latent(x, pos, g_kv, w_dkv, g_ckv)
            if past is None:
                k_nope = jnp.einsum('bsr,rhd->bshd', ckv, w_uk)
                v = jnp.einsum('bsr,rhd->bshd', ckv, w_uv)
        xn = rmsnorm(x, g_mix[l])
        if l < N_A:
            buf = jnp.zeros((B, POOL_BUF, D_MODEL), x.dtype) if pool_state is None else pool_state[l]
            y, nbuf = pool_mix(xn, buf, pos, w_pool[l], s_pool[l])
            new_pool.append(nbuf)
        else:
            j = l - N_A
            q_nope, q_pe = mla_query(xn, pos, w_dq[j], g_q[j], w_uq[j])
            if past is None:
                o = attend_prompt(q_nope, q_pe, k_nope, kpe, v, pos)
            else:
                o = attend_sample(q_nope, q_pe, ckv, kpe, past[0], past[1], past[2], w_uk, w_uv)
            y = o.reshape(B, S, N_HEADS * D_V) @ w_o[j]
        x = x + y
        x = x + moe(rmsnorm(x, g_ffn[l]), w_router[l], b_router[l], w_up[l], b_up[l], w_down[l], b_down[l])
    return rmsnorm(x, g_final), jnp.stack(new_pool, axis=0), ckv, kpe


def setup_inputs(seed: int = 0) -> dict:
    key = jax.random.key(seed)
    ks = jax.random.split(key, 32)
    f32 = jnp.float32
    n_pages = PAST_LEN // PAGE_SIZE
    n_used = DEC_BATCH * n_pages
    n_phys = n_used + max(1, n_used // 4)
    nrm = lambda k, shape, scale: jax.random.normal(k, shape, f32) * scale
    gain = lambda k, shape: 1.0 + 0.05 * jax.random.normal(k, shape, f32)
    page_table = jax.random.permutation(ks[5], n_phys)[:n_used].reshape(DEC_BATCH, n_pages).astype(jnp.int32)
    return {
        'x_prompt': nrm(ks[0], (BATCH, SEQ, D_MODEL), 1.0),
        'x_sample': nrm(ks[1], (DEC_BATCH, DEC_SEQ, D_MODEL), 1.0),
        'state_pool': nrm(ks[2], (N_A, DEC_BATCH, POOL_BUF, D_MODEL), 1.0),
        'cache_ckv': nrm(ks[3], (n_phys, PAGE_SIZE, KV_LORA), 1.0),
        'cache_kpe': nrm(ks[4], (n_phys, PAGE_SIZE, D_ROPE), 1.0),
        'page_table': page_table,
        'g_mix': gain(ks[6], (DEPTH, D_MODEL)),
        'g_ffn': gain(ks[7], (DEPTH, D_MODEL)),
        'w_pool': nrm(ks[8], (N_A, N_POOL_GROUPS, POOL_GROUP, POOL_GROUP), POOL_GROUP ** -0.5),
        's_pool': gain(ks[9], (N_A, D_MODEL)),
        'g_kv': gain(ks[10], (D_MODEL,)),
        'w_dkv': nrm(ks[11], (D_MODEL, KV_LORA + D_ROPE), D_MODEL ** -0.5),
        'g_ckv': gain(ks[12], (KV_LORA,)),
        'w_uk': nrm(ks[13], (KV_LORA, N_HEADS, D_NOPE), KV_LORA ** -0.5),
        'w_uv': nrm(ks[14], (KV_LORA, N_HEADS, D_V), KV_LORA ** -0.5),
        'w_dq': nrm(ks[15], (N_B, D_MODEL, Q_LORA), D_MODEL ** -0.5),
        'g_q': gain(ks[16], (N_B, Q_LORA)),
        'w_uq': nrm(ks[17], (N_B, Q_LORA, N_HEADS, D_QK), Q_LORA ** -0.5),
        'w_o': nrm(ks[18], (N_B, N_HEADS * D_V, D_MODEL), (N_HEADS * D_V) ** -0.5),
        'w_router': nrm(ks[19], (DEPTH, D_MODEL, N_EXPERTS), D_MODEL ** -0.5),
        'b_router': nrm(ks[20], (DEPTH, N_EXPERTS), 0.01),
        'w_up': nrm(ks[21], (DEPTH, N_EXPERTS, D_MODEL, 2 * D_EXPERT), D_MODEL ** -0.5),
        'b_up': nrm(ks[22], (DEPTH, N_EXPERTS, 2 * D_EXPERT), 0.01),
        'w_down': nrm(ks[23], (DEPTH, N_EXPERTS, D_EXPERT, D_MODEL), D_EXPERT ** -0.5),
        'b_down': nrm(ks[24], (DEPTH, N_EXPERTS, D_MODEL), 0.01),
        'g_final': gain(ks[25], (D_MODEL,)),
    }


def reference(x_prompt, x_sample, state_pool, cache_ckv, cache_kpe, page_table,
              g_mix, g_ffn, w_pool, s_pool, g_kv, w_dkv, g_ckv, w_uk, w_uv,
              w_dq, g_q, w_uq, w_o, w_router, b_router, w_up, b_up, w_down, b_down, g_final):
    params = (g_mix, g_ffn, w_pool, s_pool, g_kv, w_dkv, g_ckv, w_uk, w_uv,
              w_dq, g_q, w_uq, w_o, w_router, b_router, w_up, b_up, w_down, b_down, g_final)
    y_prompt, pool_p, ckv_p, kpe_p = _trunk(x_prompt, 0, None, None, *params)
    y_sample, pool_s, ckv_s, kpe_s = _trunk(x_sample, PAST_LEN, state_pool, (cache_ckv, cache_kpe, page_table), *params)
    return (y_prompt, y_sample, pool_p, pool_s, ckv_p, kpe_p, ckv_s, kpe_s)
```

```python
import functools

import jax
import jax.numpy as jnp
from jax import lax
from jax.experimental import pallas as pl
from jax.experimental.pallas import tpu as pltpu

POOL_WINDOWS = (2, 4, 8, 16)
POOL_BUF = max(POOL_WINDOWS) - 1
N_HEADS = 16
D_NOPE = 128
D_ROPE = 64
D_QK = D_NOPE + D_ROPE
D_V = 128
ROPE_BASE = 10000.0
ATTN_SCALE = D_QK ** -0.5
N_EXPERTS = 32
TOP_K = 4
SWIGLU_ALPHA = 1.702
SWIGLU_LIMIT = 7.0
EPS = 1e-6
PAGE_SIZE = 128

V7X_SUBLANES = 8
V7X_LANES = 128
V7X_VMEM_BYTES = 64 * 1024 * 1024
V7X_VMEM_LIMIT = 56 * 1024 * 1024

ROW_TILE = 256
POOL_HALO = 2 * V7X_SUBLANES
MOE_TM = 256
MOE_TF = 512
MOE_RB = 4
COMBINE_TT = 128
ATT_TQ = 512
ATT_TK = 512
DEC_PAGES = 16

BF16 = jnp.bfloat16
F32 = jnp.float32
I32 = jnp.int32


def _arb(n):
    return ("arbitrary",) * n


def _rms(x, g):
    return x * lax.rsqrt(jnp.mean(x * x, axis=-1, keepdims=True) + EPS) * g


def _dot(a, b):
    return jnp.dot(a, b, preferred_element_type=F32)


def _dot_nt(a, b):
    return lax.dot_general(a, b, (((1,), (1,)), ((), ())), preferred_element_type=F32)


def _pool_windows(cat_ref, xn, lead, rows, start_pos, w_ref, s_ref):
    d = xn.shape[-1]
    grp = d // len(POOL_WINDOWS)
    t_axis = xn.ndim - 2
    pos = start_pos + lax.broadcasted_iota(I32, xn.shape[:-1] + (1,), t_axis)
    outs = []
    for g, w in enumerate(POOL_WINDOWS):
        ch = slice(g * grp, (g + 1) * grp)
        acc = None
        for k in range(w):
            piece = cat_ref[lead + (pl.ds(POOL_HALO - k, rows), ch)]
            acc = piece if acc is None else acc + piece
        cnt = jnp.minimum(pos + 1, w).astype(F32)
        dg = acc / cnt - xn[..., ch]
        dg2 = dg.reshape(-1, grp).astype(BF16)
        outs.append(_dot(dg2, w_ref[g]))
    y = jnp.concatenate(outs, axis=-1)
    return y * s_ref[...]


def _pool_prompt_kernel(x_ref, g_ref, w_ref, s_ref, o_ref, np_ref, cat_ref):
    i = pl.program_id(0)
    tm = x_ref.shape[0]

    @pl.when(i == 0)
    def _():
        cat_ref[pl.ds(0, POOL_HALO), :] = jnp.zeros((POOL_HALO, cat_ref.shape[1]), F32)

    x = x_ref[...]
    xn = _rms(x, g_ref[...])
    cat_ref[pl.ds(POOL_HALO, tm), :] = xn
    y = _pool_windows(cat_ref, xn, (), tm, i * tm, w_ref, s_ref)
    o_ref[...] = x + y
    np_ref[...] = cat_ref[pl.ds(POOL_HALO + tm - POOL_BUF, POOL_BUF), :]
    cat_ref[pl.ds(0, POOL_HALO), :] = cat_ref[pl.ds(tm, POOL_HALO), :]


def _pool_prompt(x, g, w_pool_bf, s_pool):
    s, d = x.shape
    tm = ROW_TILE
    assert s % tm == 0 and tm >= POOL_HALO
    return pl.pallas_call(
        _pool_prompt_kernel,
        grid=(s // tm,),
        in_specs=[
            pl.BlockSpec((tm, d), lambda i: (i, 0)),
            pl.BlockSpec((1, d), lambda i: (0, 0)),
            pl.BlockSpec(w_pool_bf.shape, lambda i: (0, 0, 0)),
            pl.BlockSpec((1, d), lambda i: (0, 0)),
        ],
        out_specs=[
            pl.BlockSpec((tm, d), lambda i: (i, 0)),
            pl.BlockSpec((POOL_BUF, d), lambda i: (0, 0)),
        ],
        out_shape=[jax.ShapeDtypeStruct((s, d), F32), jax.ShapeDtypeStruct((POOL_BUF, d), F32)],
        scratch_shapes=[pltpu.VMEM((POOL_HALO + tm, d), F32)],
        compiler_params=pltpu.CompilerParams(dimension_semantics=_arb(1)),
        name="pool_prompt",
    )(x, g, w_pool_bf, s_pool)


def _pool_sample_kernel(start_pos, x_ref, buf_ref, g_ref, w_ref, s_ref, o_ref, np_ref, cat_ref):
    bb, sq, d = x_ref.shape
    x = x_ref[...]
    xn = _rms(x, g_ref[...])
    cat_ref[:, pl.ds(POOL_HALO - POOL_BUF, POOL_BUF), :] = buf_ref[...]
    cat_ref[:, pl.ds(POOL_HALO, sq), :] = xn
    y = _pool_windows(cat_ref, xn, (slice(None),), sq, start_pos, w_ref, s_ref)
    o_ref[...] = x + y.reshape(bb, sq, d)
    np_ref[...] = cat_ref[:, pl.ds(POOL_HALO + sq - POOL_BUF, POOL_BUF), :]


def _pool_sample(x, buf, g, w_pool_bf, s_pool, start_pos):
    b, sq, d = x.shape
    bb = 8
    assert b % bb == 0 and sq == V7X_SUBLANES
    return pl.pallas_call(
        functools.partial(_pool_sample_kernel, start_pos),
        grid=(b // bb,),
        in_specs=[
            pl.BlockSpec((bb, sq, d), lambda i: (i, 0, 0)),
            pl.BlockSpec((bb, POOL_BUF, d), lambda i: (i, 0, 0)),
            pl.BlockSpec((1, d), lambda i: (0, 0)),
            pl.BlockSpec(w_pool_bf.shape, lambda i: (0, 0, 0)),
            pl.BlockSpec((1, d), lambda i: (0, 0)),
        ],
        out_specs=[
            pl.BlockSpec((bb, sq, d), lambda i: (i, 0, 0)),
            pl.BlockSpec((bb, POOL_BUF, d), lambda i: (i, 0, 0)),
        ],
        out_shape=[jax.ShapeDtypeStruct((b, sq, d), F32), jax.ShapeDtypeStruct((b, POOL_BUF, d), F32)],
        scratch_shapes=[pltpu.VMEM((bb, POOL_HALO + sq, d), F32)],
        compiler_params=pltpu.CompilerParams(dimension_semantics=_arb(1)),
        name="pool_sample",
    )(x, buf, g, w_pool_bf, s_pool)


def _router_kernel(x_ref, g_ref, wr_ref, br_ref, xn_ref, e_ref, gate_ref):
    xn = _rms(x_ref[...], g_ref[...])
    xn_ref[...] = xn
    logits = jnp.dot(xn, wr_ref[...], preferred_element_type=F32,
                     precision=lax.Precision.HIGHEST) + br_ref[...]
    tm, ne = logits.shape
    lane = lax.broadcasted_iota(I32, (tm, ne), 1)
    kcol = lax.broadcasted_iota(I32, (tm, TOP_K), 1)
    work = logits
    idx_out = jnp.zeros((tm, TOP_K), I32)
    val_out = jnp.zeros((tm, TOP_K), F32)
    for k in range(TOP_K):
        m = jnp.max(work, axis=-1, keepdims=True)
        idx = jnp.min(jnp.where(work == m, lane, ne), axis=-1, keepdims=True)
        idx_out = jnp.where(kcol == k, idx, idx_out)
        val_out = jnp.where(kcol == k, m, val_out)
        work = jnp.where(lane == idx, -jnp.inf, work)
    ex = jnp.exp(val_out - val_out[:, 0:1])
    e_ref[...] = idx_out
    gate_ref[...] = ex / jnp.sum(ex, axis=-1, keepdims=True)


def _router(x, g, w_router, b_router):
    t, d = x.shape
    tm = ROW_TILE
    ne = w_router.shape[1]
    return pl.pallas_call(
        _router_kernel,
        grid=(t // tm,),
        in_specs=[
            pl.BlockSpec((tm, d), lambda i: (i, 0)),
            pl.BlockSpec((1, d), lambda i: (0, 0)),
            pl.BlockSpec((d, ne), lambda i: (0, 0)),
            pl.BlockSpec((1, ne), lambda i: (0, 0)),
        ],
        out_specs=[
            pl.BlockSpec((tm, d), lambda i: (i, 0)),
            pl.BlockSpec((tm, TOP_K), lambda i: (i, 0)),
            pl.BlockSpec((tm, TOP_K), lambda i: (i, 0)),
        ],
        out_shape=[
            jax.ShapeDtypeStruct((t, d), F32),
            jax.ShapeDtypeStruct((t, TOP_K), I32),
            jax.ShapeDtypeStruct((t, TOP_K), F32),
        ],
        compiler_params=pltpu.CompilerParams(dimension_semantics=_arb(1)),
        name="router",
    )(x, g, w_router, b_router)


def _moe_tables(e4, t, nb_max, nf):
    tm, rb = MOE_TM, MOE_RB
    ne = N_EXPERTS
    sel = (e4[:, :, None] == jnp.arange(ne, dtype=I32)[None, None, :]).any(axis=1)
    seli = sel.astype(I32)
    counts = seli.sum(axis=0)
    nb = (counts + tm - 1) // tm
    bend = jnp.cumsum(nb)
    bstart = bend - nb
    rank = jnp.cumsum(seli, axis=0) - seli
    dest = bstart[None, :] * tm + rank
    pos4 = jnp.take_along_axis(dest, e4, axis=1).astype(I32)
    n_rows = nb_max * tm
    tok = jnp.broadcast_to(jnp.arange(t, dtype=I32)[:, None], (t, TOP_K))
    row_tok = jnp.zeros((n_rows,), I32).at[pos4.reshape(-1)].set(tok.reshape(-1), unique_indices=True)
    ns = nb_max * nf
    steps_e = nf * nb
    send = jnp.cumsum(steps_e)
    sstart = send - steps_e
    total = send[-1]
    s = jnp.arange(ns, dtype=I32)
    sc = jnp.minimum(s, total - 1)
    e = jnp.minimum(jnp.searchsorted(send, sc, side='right').astype(I32), ne - 1)
    local = sc - sstart[e]
    per_group = nf * rb
    gi = local // per_group
    rem = local - gi * per_group
    gnb = jnp.minimum(rb, nb[e] - gi * rb)
    j = rem // gnb
    r = rem - j * gnb
    gfirst = bstart[e] + gi * rb
    blk = gfirst + r
    oblk = jnp.where(j == nf - 1, blk, gfirst)
    n_used = bend[-1]
    spare = s - total
    fill = jnp.logical_and(spare >= 0, spare < nb_max - n_used)
    oblk = jnp.where(s < total, oblk, jnp.minimum(n_used + jnp.maximum(spare, 0), nb_max - 1))
    flag = (s < total).astype(I32) + 2 * (r == 0).astype(I32) + 4 * fill.astype(I32)
    nbu = bend[-1:].astype(I32)
    return dict(pos4=pos4, row_tok=row_tok, nbu=nbu,
                st_e=e.astype(I32), st_j=j.astype(I32), st_blk=blk.astype(I32),
                st_oblk=oblk.astype(I32), st_r=r.astype(I32), st_flag=flag)


def _row_copy(src_hbm, dst, src_row, dst_row, sem):
    return pltpu.make_async_copy(src_hbm.at[pl.ds(src_row, 1)], dst.at[pl.ds(dst_row, 1)], sem)


def _gather_kernel(nbu_ref, tok_vmem, x_hbm, o_hbm, tok_smem, sem_idx, sem):
    i = pl.program_id(0)
    tm = tok_smem.shape[1]

    @pl.when(i < nbu_ref[0])
    def _():
        cp = pltpu.make_async_copy(tok_vmem.at[0], tok_smem, sem_idx)
        cp.start()
        cp.wait()

        def issue(r, c):
            _row_copy(x_hbm, o_hbm, tok_smem[0, r], i * tm + r, sem).start()
            return c
        lax.fori_loop(0, tm, issue, 0, unroll=8)
        pltpu.make_async_copy(x_hbm.at[pl.ds(0, tm)], o_hbm.at[pl.ds(i * tm, tm)], sem).wait()

    @pl.when(i >= nbu_ref[0])
    def _():
        cp = pltpu.make_async_copy(x_hbm.at[pl.ds(0, tm)], o_hbm.at[pl.ds(i * tm, tm)], sem)
        cp.start()
        cp.wait()


def _dispatch_gather(xn, row_tok, nbu, nb_max):
    t, d = xn.shape
    tm = MOE_TM
    tok3 = row_tok.reshape(nb_max, 1, tm)
    return pl.pallas_call(
        _gather_kernel,
        grid_spec=pltpu.PrefetchScalarGridSpec(
            num_scalar_prefetch=1,
            grid=(nb_max,),
            in_specs=[
                pl.BlockSpec((1, 1, tm), lambda i, nbu: (i, 0, 0)),
                pl.BlockSpec(memory_space=pl.ANY),
            ],
            out_specs=pl.BlockSpec(memory_space=pl.ANY),
            scratch_shapes=[pltpu.SMEM((1, tm), I32), pltpu.SemaphoreType.DMA(()), pltpu.SemaphoreType.DMA(())],
        ),
        out_shape=jax.ShapeDtypeStruct((nb_max * tm, d), F32),
        compiler_params=pltpu.CompilerParams(dimension_semantics=_arb(1)),
        name="moe_dispatch",
    )(nbu, tok3, xn)


def _expert_kernel(nf, st_e, st_j, st_blk, st_oblk, st_r, st_flag,
                   x_ref, wg_ref, wl_ref, bg_ref, bl_ref, wd_ref, bd_ref, o_ref,
                   wg_s, wl_s, wd_s, acc_s):
    s = pl.program_id(0)
    flag = st_flag[s]
    j = st_j[s]
    r = st_r[s]

    @pl.when((flag & 4) != 0)
    def _():
        o_ref[...] = jnp.zeros(o_ref.shape, o_ref.dtype)

    @pl.when((flag & 1) != 0)
    def _():
        @pl.when((flag & 2) != 0)
        def _():
            wg_s[...] = wg_ref[...].astype(BF16)
            wl_s[...] = wl_ref[...].astype(BF16)
            wd_s[...] = wd_ref[...].astype(BF16)

        x = x_ref[...].astype(BF16)
        hg = _dot(x, wg_s[...]) + bg_ref[...]
        hl = _dot(x, wl_s[...]) + bl_ref[...]
        hg = jnp.minimum(hg, SWIGLU_LIMIT)
        hl = jnp.clip(hl, -SWIGLU_LIMIT, SWIGLU_LIMIT)
        a = hg * jax.nn.sigmoid(SWIGLU_ALPHA * hg) * (hl + 1.0)
        c = _dot(a.astype(BF16), wd_s[...])

        if nf == 1:
            o_ref[...] = c + bd_ref[...]
        else:
            @pl.when(j == 0)
            def _():
                acc_s[r] = c + bd_ref[...]

            @pl.when(jnp.logical_and(j > 0, j < nf - 1))
            def _():
                acc_s[r] += c

            @pl.when(j == nf - 1)
            def _():
                o_ref[...] = acc_s[r] + c


def _expert_mlp(x_sorted, tabs, w_up, b_up, w_down, b_down, nb_max):
    n_rows, d = x_sorted.shape
    ne, _, f2 = w_up.shape
    f = f2 // 2
    tm, tf, rb = MOE_TM, MOE_TF, MOE_RB
    nf = f // tf
    ns = nb_max * nf
    b_up3 = b_up.reshape(ne, 1, f2)
    b_down3 = b_down.reshape(ne, 1, d)
    vmem = (2 * 3 * d * tf * 4) + (3 * d * tf * 2) + rb * tm * d * 4 + 2 * tm * d * 4 + 2 * tm * d * 4 + (4 << 20)
    assert vmem <= V7X_VMEM_LIMIT, vmem
    return pl.pallas_call(
        functools.partial(_expert_kernel, nf),
        grid_spec=pltpu.PrefetchScalarGridSpec(
            num_scalar_prefetch=6,
            grid=(ns,),
            in_specs=[
                pl.BlockSpec((tm, d), lambda s, e, j, b, ob, r, fl: (b[s], 0)),
                pl.BlockSpec((None, d, tf), lambda s, e, j, b, ob, r, fl: (e[s], 0, j[s])),
                pl.BlockSpec((None, d, tf), lambda s, e, j, b, ob, r, fl: (e[s], 0, nf + j[s])),
                pl.BlockSpec((None, 1, tf), lambda s, e, j, b, ob, r, fl: (e[s], 0, j[s])),
                pl.BlockSpec((None, 1, tf), lambda s, e, j, b, ob, r, fl: (e[s], 0, nf + j[s])),
                pl.BlockSpec((None, tf, d), lambda s, e, j, b, ob, r, fl: (e[s], j[s], 0)),
                pl.BlockSpec((None, 1, d), lambda s, e, j, b, ob, r, fl: (e[s], 0, 0)),
            ],
            out_specs=pl.BlockSpec((tm, d), lambda s, e, j, b, ob, r, fl: (ob[s], 0)),
            scratch_shapes=[
                pltpu.VMEM((d, tf), BF16), pltpu.VMEM((d, tf), BF16), pltpu.VMEM((tf, d), BF16),
                pltpu.VMEM((rb, tm, d), F32),
            ],
        ),
        out_shape=jax.ShapeDtypeStruct((n_rows, d), F32),
        compiler_params=pltpu.CompilerParams(dimension_semantics=_arb(1), vmem_limit_bytes=vmem),
        name="moe_experts",
    )(tabs['st_e'], tabs['st_j'], tabs['st_blk'], tabs['st_oblk'], tabs['st_r'], tabs['st_flag'],
      x_sorted, w_up, w_up, b_up3, b_up3, w_down, b_down3)


def _combine_kernel(final_norm, pos_vmem, g_ref, x_ref, gf_ref, y_hbm, o_ref, pos_smem, ybuf, sem_idx, sem):
    tt = x_ref.shape[0]
    cp = pltpu.make_async_copy(pos_vmem.at[0], pos_smem, sem_idx)
    cp.start()
    cp.wait()
    for k in range(TOP_K):
        def issue(r, c, k=k):
            _row_copy(y_hbm, ybuf.at[k], pos_smem[0, k * tt + r], r, sem).start()
            return c
        lax.fori_loop(0, tt, issue, 0, unroll=8)
    for k in range(TOP_K):
        pltpu.make_async_copy(y_hbm.at[pl.ds(0, tt)], ybuf.at[k], sem).wait()
    g = g_ref[...]
    acc = x_ref[...]
    for k in range(TOP_K):
        acc = acc + g[:, k:k + 1] * ybuf[k]
    if final_norm:
        acc = _rms(acc, gf_ref[...])
    o_ref[...] = acc


def _combine(x, y_sorted, pos4, gates, g_final, final_norm):
    t, d = x.shape
    tt = COMBINE_TT
    nt = t // tt
    pos3 = pos4.reshape(nt, tt, TOP_K).transpose(0, 2, 1).reshape(nt, 1, TOP_K * tt)
    return pl.pallas_call(
        functools.partial(_combine_kernel, final_norm),
        grid=(nt,),
        in_specs=[
            pl.BlockSpec((1, 1, TOP_K * tt), lambda i: (i, 0, 0)),
            pl.BlockSpec((tt, TOP_K), lambda i: (i, 0)),
            pl.BlockSpec((tt, d), lambda i: (i, 0)),
            pl.BlockSpec((1, d), lambda i: (0, 0)),
            pl.BlockSpec(memory_space=pl.ANY),
        ],
        out_specs=pl.BlockSpec((tt, d), lambda i: (i, 0)),
        out_shape=jax.ShapeDtypeStruct((t, d), F32),
        scratch_shapes=[
            pltpu.SMEM((1, TOP_K * tt), I32),
            pltpu.VMEM((TOP_K, tt, d), F32),
            pltpu.SemaphoreType.DMA(()), pltpu.SemaphoreType.DMA(()),
        ],
        compiler_params=pltpu.CompilerParams(dimension_semantics=_arb(1)),
        name="moe_combine",
    )(pos3, gates, x, g_final, y_sorted)


def _moe_layer(x, g_ffn, w_router, b_router, w_up, b_up, w_down, b_down, g_final, final_norm):
    t, d = x.shape
    f = w_up.shape[2] // 2
    nf = f // MOE_TF
    nb_max = -(-(t * TOP_K) // MOE_TM) + N_EXPERTS
    xn, e4, gates = _router(x, g_ffn, w_router, b_router)
    tabs = _moe_tables(e4, t, nb_max, nf)
    x_sorted = _dispatch_gather(xn, tabs['row_tok'], tabs['nbu'], nb_max)
    y_sorted = _expert_mlp(x_sorted, tabs, w_up, b_up, w_down, b_down, nb_max)
    return _combine(x, y_sorted, tabs['pos4'], gates, g_final, final_norm)


def _proj_kernel(x_ref, cos_ref, sin_ref, gkv_ref, gmix_ref, gckv_ref, gq_ref,
                 wc_ref, wp_ref, wpr_ref, wdq_ref, wqn_ref, wqp_ref, wqpr_ref,
                 ckv_ref, kpe_ref, q_ref):
    x = x_ref[...]
    cos = cos_ref[...]
    sin = sin_ref[...]
    xkv = _rms(x, gkv_ref[...]).astype(BF16)
    ckv_ref[...] = _rms(_dot(xkv, wc_ref[...]), gckv_ref[...])
    kpe_ref[...] = _dot(xkv, wp_ref[...]) * cos + _dot(xkv, wpr_ref[...]) * sin
    xq = _rms(x, gmix_ref[...]).astype(BF16)
    cq = _rms(_dot(xq, wdq_ref[...]), gq_ref[...]).astype(BF16)
    qn = _dot(cq, wqn_ref[...]) * ATTN_SCALE
    qp = _dot(cq, wqp_ref[...])
    qpr = _dot(cq, wqpr_ref[...])
    for h in range(N_HEADS):
        q_ref[h, :, 0:D_NOPE] = qn[:, h * D_NOPE:(h + 1) * D_NOPE].astype(BF16)
        ph = qp[:, h * D_ROPE:(h + 1) * D_ROPE] * cos + qpr[:, h * D_ROPE:(h + 1) * D_ROPE] * sin
        q_ref[h, :, D_NOPE:D_QK] = (ph * ATTN_SCALE).astype(BF16)


def _proj(x, cos, sin, g_kv, g_mix, g_ckv, g_q, wc, wp, wpr, wdq, wqn, wqp, wqpr):
    t, d = x.shape
    tm = ROW_TILE
    kvl = wc.shape[1]
    full = lambda a: pl.BlockSpec(a.shape, lambda i: (0,) * a.ndim)
    row = lambda n: pl.BlockSpec((tm, n), lambda i: (i, 0))
    consts = (g_kv, g_mix, g_ckv, g_q, wc, wp, wpr, wdq, wqn, wqp, wqpr)
    return pl.pallas_call(
        _proj_kernel,
        grid=(t // tm,),
        in_specs=[row(d), row(D_ROPE), row(D_ROPE)] + [full(a) for a in consts],
        out_specs=[row(kvl), row(D_ROPE), pl.BlockSpec((N_HEADS, tm, D_QK), lambda i: (0, i, 0))],
        out_shape=[
            jax.ShapeDtypeStruct((t, kvl), F32),
            jax.ShapeDtypeStruct((t, D_ROPE), F32),
            jax.ShapeDtypeStruct((N_HEADS, t, D_QK), BF16),
        ],
        compiler_params=pltpu.CompilerParams(dimension_semantics=_arb(1), vmem_limit_bytes=48 << 20),
        name="latent_q_proj",
    )(x, cos, sin, *consts)


def _kv_up_kernel(ckv_ref, kpe_ref, wuk_ref, wuv_ref, k_ref, v_ref):
    c = ckv_ref[...].astype(BF16)
    kn = _dot(c, wuk_ref[...])
    vv = _dot(c, wuv_ref[...])
    kp = kpe_ref[...].astype(BF16)
    for h in range(N_HEADS):
        k_ref[h, :, 0:D_NOPE] = kn[:, h * D_NOPE:(h + 1) * D_NOPE].astype(BF16)
        k_ref[h, :, D_NOPE:D_QK] = kp
        v_ref[h] = vv[:, h * D_V:(h + 1) * D_V].astype(BF16)


def _kv_up(ckv, kpe, s, wuk2, wuv2):
    tm = ROW_TILE
    kvl = ckv.shape[1]
    return pl.pallas_call(
        _kv_up_kernel,
        grid=(s // tm,),
        in_specs=[
            pl.BlockSpec((tm, kvl), lambda i: (i, 0)),
            pl.BlockSpec((tm, D_ROPE), lambda i: (i, 0)),
            pl.BlockSpec(wuk2.shape, lambda i: (0, 0)),
            pl.BlockSpec(wuv2.shape, lambda i: (0, 0)),
        ],
        out_specs=[
            pl.BlockSpec((N_HEADS, tm, D_QK), lambda i: (0, i, 0)),
            pl.BlockSpec((N_HEADS, tm, D_V), lambda i: (0, i, 0)),
        ],
        out_shape=[
            jax.ShapeDtypeStruct((N_HEADS, s, D_QK), BF16),
            jax.ShapeDtypeStruct((N_HEADS, s, D_V), BF16),
        ],
        compiler_params=pltpu.CompilerParams(dimension_semantics=_arb(1)),
        name="kv_up",
    )(ckv, kpe, wuk2, wuv2)


def _flash_kernel(q_ref, k_ref, v_ref, o_ref, m_s, l_s, acc_s):
    i = pl.program_id(1)
    tq = q_ref.shape[0]
    tk = ATT_TK
    q = q_ref[...]
    m_s[...] = jnp.full(m_s.shape, -jnp.inf, F32)
    l_s[...] = jnp.zeros(l_s.shape, F32)
    acc_s[...] = jnp.zeros(acc_s.shape, F32)

    def update(s, vblk):
        m_prev = m_s[...]
        m_new = jnp.maximum(m_prev, jnp.max(s, axis=-1, keepdims=True))
        corr = jnp.exp(m_prev - m_new)
        p = jnp.exp(s - m_new)
        l_s[...] = l_s[...] * corr + jnp.sum(p, axis=-1, keepdims=True)
        acc_s[...] = acc_s[...] * corr + _dot(p.astype(BF16), vblk)
        m_s[...] = m_new

    def body(j, c):
        off = pl.multiple_of(j * tk, tk)
        s = _dot_nt(q, k_ref[pl.ds(off, tk), :])
        update(s, v_ref[pl.ds(off, tk), :])
        return c

    nfull = (i * tq) // tk
    lax.fori_loop(0, nfull, body, 0)
    for dj in range(tq // tk):
        off = pl.multiple_of(i * tq + dj * tk, tk)
        s = _dot_nt(q, k_ref[pl.ds(off, tk), :])
        row = lax.broadcasted_iota(I32, (tq, tk), 0)
        col = lax.broadcasted_iota(I32, (tq, tk), 1) + dj * tk
        s = jnp.where(col <= row, s, -jnp.inf)
        update(s, v_ref[pl.ds(off, tk), :])
    o_ref[...] = (acc_s[...] / l_s[...]).astype(o_ref.dtype)


def _flash_prompt(q_cat, k_cat, v, s):
    tq = ATT_TQ
    assert tq % ATT_TK == 0 and s % tq == 0
    return pl.pallas_call(
        _flash_kernel,
        grid=(N_HEADS, s // tq),
        in_specs=[
            pl.BlockSpec((None, tq, D_QK), lambda h, i: (h, i, 0)),
            pl.BlockSpec((None, s, D_QK), lambda h, i: (h, 0, 0)),
            pl.BlockSpec((None, s, D_V), lambda h, i: (h, 0, 0)),
        ],
        out_specs=pl.BlockSpec((tq, D_V), lambda h, i: (i, h)),
        out_shape=jax.ShapeDtypeStruct((s, N_HEADS * D_V), BF16),
        scratch_shapes=[pltpu.VMEM((tq, 1), F32), pltpu.VMEM((tq, 1), F32), pltpu.VMEM((tq, D_V), F32)],
        compiler_params=pltpu.CompilerParams(dimension_semantics=_arb(2), vmem_limit_bytes=40 << 20),
        name="flash_prompt",
    )(q_cat, k_cat, v)


def _q_absorb_kernel(q_ref, wukt_ref, qlat_ref, qpe_ref):
    q = q_ref[...]
    qlat_ref[...] = _dot(q[:, 0:D_NOPE], wukt_ref[...])
    qpe_ref[...] = q[:, D_NOPE:D_QK].astype(F32)


def _q_absorb(q_cat, row0, ts, wukt):
    kvl = wukt.shape[2]
    assert row0 % ts == 0
    rb = row0 // ts
    return pl.pallas_call(
        _q_absorb_kernel,
        grid=(N_HEADS,),
        in_specs=[
            pl.BlockSpec((None, ts, D_QK), lambda h: (h, rb, 0)),
            pl.BlockSpec((None, D_NOPE, kvl), lambda h: (h, 0, 0)),
        ],
        out_specs=[
            pl.BlockSpec((None, ts, kvl), lambda h: (h, 0, 0)),
            pl.BlockSpec((None, ts, D_ROPE), lambda h: (h, 0, 0)),
        ],
        out_shape=[
            jax.ShapeDtypeStruct((N_HEADS, ts, kvl), F32),
            jax.ShapeDtypeStruct((N_HEADS, ts, D_ROPE), F32),
        ],
        compiler_params=pltpu.CompilerParams(dimension_semantics=_arb(1)),
        name="q_absorb",
    )(q_cat, wukt)


def _dec_attn_kernel(n_chunks, pt_ref, qlat_ref, qpe_ref, cnew_ref, pnew_ref, ckv_hbm, kpe_hbm, o_ref,
                     cbuf, pbuf, sem, m_s, l_s, acc_s):
    b = pl.program_id(0)
    c = pl.program_id(1)
    nb = pl.num_programs(0)
    n = b * n_chunks + c
    slot = lax.rem(n, 2)
    npg = DEC_PAGES
    nh, sq, kvl = qlat_ref.shape
    rows = nh * sq

    def page_copies(step, slot_, i):
        bb = step // n_chunks
        cc = step - bb * n_chunks
        page = pt_ref[bb, cc * npg + i]
        dst = pl.ds(i * PAGE_SIZE, PAGE_SIZE)
        return (pltpu.make_async_copy(ckv_hbm.at[page], cbuf.at[slot_, dst], sem.at[slot_]),
                pltpu.make_async_copy(kpe_hbm.at[page], pbuf.at[slot_, dst], sem.at[slot_]))

    def fetch(step, slot_):
        for i in range(npg):
            for cp in page_copies(step, slot_, i):
                cp.start()

    @pl.when(n == 0)
    def _():
        fetch(n, slot)

    @pl.when(n + 1 < nb * n_chunks)
    def _():
        fetch(n + 1, 1 - slot)

    q_l = qlat_ref[...].reshape(rows, kvl).astype(BF16)
    q_p = qpe_ref[...].reshape(rows, D_ROPE).astype(BF16)

    @pl.when(c == 0)
    def _():
        cn = cnew_ref[...].astype(BF16)
        pn = pnew_ref[...].astype(BF16)
        s = _dot_nt(q_l, cn) + _dot_nt(q_p, pn)
        srow = lax.rem(lax.broadcasted_iota(I32, (rows, sq), 0), sq)
        tcol = lax.broadcasted_iota(I32, (rows, sq), 1)
        s = jnp.where(srow >= tcol, s, -jnp.inf)
        m = jnp.max(s, axis=-1, keepdims=True)
        p = jnp.exp(s - m)
        m_s[...] = m
        l_s[...] = jnp.sum(p, axis=-1, keepdims=True)
        acc_s[...] = _dot(p.astype(BF16), cn)

    for i in range(npg):
        for cp in page_copies(n, slot, i):
            cp.wait()

    ck = cbuf[slot].astype(BF16)
    kp = pbuf[slot].astype(BF16)
    s = _dot_nt(q_l, ck) + _dot_nt(q_p, kp)
    m_prev = m_s[...]
    m_new = jnp.maximum(m_prev, jnp.max(s, axis=-1, keepdims=True))
    corr = jnp.exp(m_prev - m_new)
    p = jnp.exp(s - m_new)
    l_new = l_s[...] * corr + jnp.sum(p, axis=-1, keepdims=True)
    acc_new = acc_s[...] * corr + _dot(p.astype(BF16), ck)
    m_s[...] = m_new
    l_s[...] = l_new
    acc_s[...] = acc_new

    @pl.when(c == n_chunks - 1)
    def _():
        o_ref[...] = (acc_new / l_new).reshape(nh, sq, kvl)


def _dec_attn(q_lat, q_pe, ckv_new, kpe_new, cache_ckv, cache_kpe, page_table):
    nh, ts, kvl = q_lat.shape
    bsz, n_pages = page_table.shape
    sq = ts // bsz
    assert n_pages % DEC_PAGES == 0 and sq == V7X_SUBLANES
    n_chunks = n_pages // DEC_PAGES
    rows = nh * sq
    keys = DEC_PAGES * PAGE_SIZE
    return pl.pallas_call(
        functools.partial(_dec_attn_kernel, n_chunks),
        grid_spec=pltpu.PrefetchScalarGridSpec(
            num_scalar_prefetch=1,
            grid=(bsz, n_chunks),
            in_specs=[
                pl.BlockSpec((nh, sq, kvl), lambda b, c, pt: (0, b, 0)),
                pl.BlockSpec((nh, sq, D_ROPE), lambda b, c, pt: (0, b, 0)),
                pl.BlockSpec((sq, kvl), lambda b, c, pt: (b, 0)),
                pl.BlockSpec((sq, D_ROPE), lambda b, c, pt: (b, 0)),
                pl.BlockSpec(memory_space=pl.ANY),
                pl.BlockSpec(memory_space=pl.ANY),
            ],
            out_specs=pl.BlockSpec((nh, sq, kvl), lambda b, c, pt: (0, b, 0)),
            scratch_shapes=[
                pltpu.VMEM((2, keys, kvl), F32),
                pltpu.VMEM((2, keys, D_ROPE), F32),
                pltpu.SemaphoreType.DMA((2,)),
                pltpu.VMEM((rows, 1), F32), pltpu.VMEM((rows, 1), F32), pltpu.VMEM((rows, kvl), F32),
            ],
        ),
        out_shape=jax.ShapeDtypeStruct((nh, ts, kvl), F32),
        compiler_params=pltpu.CompilerParams(dimension_semantics=_arb(2), vmem_limit_bytes=40 << 20),
        name="dec_attn",
    )(page_table, q_lat, q_pe, ckv_new, kpe_new, cache_ckv, cache_kpe)


def _v_up_kernel(ol_ref, wuv_ref, o_ref):
    o_ref[...] = _dot(ol_ref[...].astype(BF16), wuv_ref[...]).astype(o_ref.dtype)


def _v_up(o_lat, wuv_h):
    nh, ts, kvl = o_lat.shape
    return pl.pallas_call(
        _v_up_kernel,
        grid=(nh,),
        in_specs=[
            pl.BlockSpec((None, ts, kvl), lambda h: (h, 0, 0)),
            pl.BlockSpec((None, kvl, D_V), lambda h: (h, 0, 0)),
        ],
        out_specs=pl.BlockSpec((ts, D_V), lambda h: (0, h)),
        out_shape=jax.ShapeDtypeStruct((ts, nh * D_V), BF16),
        compiler_params=pltpu.CompilerParams(dimension_semantics=_arb(1)),
        name="v_up",
    )(o_lat, wuv_h)


def _oproj_kernel(o_ref, x_ref, w_ref, y_ref):
    y_ref[...] = x_ref[...] + _dot(o_ref[...], w_ref[...])


def _oproj(o, x, w_o_bf):
    t, d = x.shape
    tm = ROW_TILE
    return pl.pallas_call(
        _oproj_kernel,
        grid=(t // tm,),
        in_specs=[
            pl.BlockSpec((tm, o.shape[1]), lambda i: (i, 0)),
            pl.BlockSpec((tm, d), lambda i: (i, 0)),
            pl.BlockSpec(w_o_bf.shape, lambda i: (0, 0)),
        ],
        out_specs=pl.BlockSpec((tm, d), lambda i: (i, 0)),
        out_shape=jax.ShapeDtypeStruct((t, d), F32),
        compiler_params=pltpu.CompilerParams(dimension_semantics=_arb(1), vmem_limit_bytes=40 << 20),
        name="attn_out_proj",
    )(o, x, w_o_bf)


def _rot_half_cols(w):
    half = D_ROPE // 2
    return jnp.concatenate([-w[..., half:], w[..., :half]], axis=-1)


def _rope_tables(pos):
    half = D_ROPE // 2
    inv_freq = jnp.power(ROPE_BASE, -jnp.arange(half, dtype=F32) / half)
    ang = pos.astype(F32)[:, None] * inv_freq[None, :]
    cos = jnp.cos(ang)
    sin = jnp.sin(ang)
    return jnp.concatenate([cos, cos], axis=-1), jnp.concatenate([sin, sin], axis=-1)


def kernel(x_prompt, x_sample, state_pool, cache_ckv, cache_kpe, page_table, g_mix, g_ffn, w_pool, s_pool,
           g_kv, w_dkv, g_ckv, w_uk, w_uv, w_dq, g_q, w_uq, w_o, w_router, b_router, w_up, b_up,
           w_down, b_down, g_final):
    bp, s, d = x_prompt.shape
    bd, sq, _ = x_sample.shape
    assert bp == 1 and g_mix.shape[0] == 2 and state_pool.shape[0] == 1
    ts = bd * sq
    t = s + ts
    past_len = page_table.shape[1] * cache_ckv.shape[1]
    kvl = w_uk.shape[0]
    row2 = lambda v: v.reshape(1, -1)

    w_pool_bf = w_pool[0].astype(BF16)
    x1p, pool_p = _pool_prompt(x_prompt[0], row2(g_mix[0]), w_pool_bf, row2(s_pool[0]))
    x1s, pool_s = _pool_sample(x_sample, state_pool[0], row2(g_mix[0]), w_pool_bf, row2(s_pool[0]), past_len)
    x1 = jnp.concatenate([x1p, x1s.reshape(ts, d)], axis=0)

    x2 = _moe_layer(x1, row2(g_ffn[0]), w_router[0], row2(b_router[0]), w_up[0], b_up[0], w_down[0], b_down[0],
                    row2(g_final), False)

    pos = jnp.concatenate([jnp.arange(s, dtype=I32),
                           jnp.tile(past_len + jnp.arange(sq, dtype=I32), bd)])
    cos, sin = _rope_tables(pos)
    w_dkv_c = w_dkv[:, :kvl].astype(BF16)
    w_dkv_p = w_dkv[:, kvl:]
    w_uq1 = w_uq[0]
    ql = w_uq1.shape[0]
    w_qn = w_uq1[:, :, :D_NOPE].reshape(ql, N_HEADS * D_NOPE).astype(BF16)
    w_qp = w_uq1[:, :, D_NOPE:]
    ckv, kpe, q_cat = _proj(
        x2, cos, sin, row2(g_kv), row2(g_mix[1]), row2(g_ckv), row2(g_q[0]),
        w_dkv_c, w_dkv_p.astype(BF16), _rot_half_cols(w_dkv_p).astype(BF16), w_dq[0].astype(BF16),
        w_qn, w_qp.reshape(ql, N_HEADS * D_ROPE).astype(BF16),
        _rot_half_cols(w_qp).reshape(ql, N_HEADS * D_ROPE).astype(BF16))

    wuk2 = w_uk.reshape(kvl, N_HEADS * D_NOPE).astype(BF16)
    wuv2 = w_uv.reshape(kvl, N_HEADS * D_V).astype(BF16)
    k_cat, v = _kv_up(ckv, kpe, s, wuk2, wuv2)
    o_p = _flash_prompt(q_cat, k_cat, v, s)

    wukt = jnp.transpose(w_uk, (1, 2, 0)).astype(BF16)
    wuv_h = jnp.transpose(w_uv, (1, 0, 2)).astype(BF16)
    q_lat, q_pe = _q_absorb(q_cat, s, ts, wukt)
    ckv_s = ckv[s:]
    kpe_s = kpe[s:]
    o_lat = _dec_attn(q_lat, q_pe, ckv_s, kpe_s, cache_ckv, cache_kpe, page_table)
    o_s = _v_up(o_lat, wuv_h)

    o = jnp.concatenate([o_p, o_s], axis=0)
    x3 = _oproj(o, x2, w_o[0].astype(BF16))
    y = _moe_layer(x3, row2(g_ffn[1]), w_router[1], row2(b_router[1]), w_up[1], b_up[1], w_down[1], b_down[1],
                   row2(g_final), True)

    return (y[:s].reshape(1, s, d), y[s:].reshape(bd, sq, d),
            pool_p.reshape(1, 1, POOL_BUF, d), pool_s.reshape(1, bd, POOL_BUF, d),
            ckv[:s].reshape(1, s, kvl), kpe[:s].reshape(1, s, D_ROPE),
            ckv_s.reshape(bd, sq, kvl), kpe_s.reshape(bd, sq, D_ROPE))
```

```python
import functools

import jax
import jax.numpy as jnp
from jax import lax
from jax.experimental import pallas as pl
from jax.experimental.pallas import tpu as pltpu

POOL_WINDOWS = (2, 4, 8, 16)
POOL_BUF = max(POOL_WINDOWS) - 1
N_HEADS = 16
D_NOPE = 128
D_ROPE = 64
D_QK = D_NOPE + D_ROPE
D_V = 128
ROPE_BASE = 10000.0
ATTN_SCALE = D_QK ** -0.5
N_EXPERTS = 32
TOP_K = 4
SWIGLU_ALPHA = 1.702
SWIGLU_LIMIT = 7.0
EPS = 1e-6
PAGE_SIZE = 128

V7X_SUBLANES = 8
V7X_LANES = 128
V7X_VMEM_BYTES = 64 * 1024 * 1024
V7X_VMEM_LIMIT = 56 * 1024 * 1024

ROW_TILE = 256
POOL_HALO = 2 * V7X_SUBLANES
MOE_TM = 256
MOE_TF = 512
MOE_RB = 8
COMBINE_TT = 128
ATT_TQ = 512
ATT_TK = 512
ATT_HEAD_GROUP = 4
DEC_PAGES = 32
DEC_SUB_PAGES = 4

BF16 = jnp.bfloat16
F32 = jnp.float32
I32 = jnp.int32


def _arb(n):
    return ("arbitrary",) * n


def _rms(x, g):
    return x * lax.rsqrt(jnp.mean(x * x, axis=-1, keepdims=True) + EPS) * g


def _dot(a, b):
    return jnp.dot(a, b, preferred_element_type=F32)


def _dot_nt(a, b):
    return lax.dot_general(a, b, (((1,), (1,)), ((), ())), preferred_element_type=F32)


def _dot_tn(a, b):
    return lax.dot_general(a, b, (((0,), (0,)), ((), ())), preferred_element_type=F32)


def _pool_windows(cat_ref, xn, lead, rows, start_pos, w_ref, s_ref):
    d = xn.shape[-1]
    grp = d // len(POOL_WINDOWS)
    t_axis = xn.ndim - 2
    pos = start_pos + lax.broadcasted_iota(I32, xn.shape[:-1] + (1,), t_axis)
    outs = []
    for g, w in enumerate(POOL_WINDOWS):
        ch = slice(g * grp, (g + 1) * grp)
        acc = None
        for k in range(w):
            piece = cat_ref[lead + (pl.ds(POOL_HALO - k, rows), ch)]
            acc = piece if acc is None else acc + piece
        cnt = jnp.minimum(pos + 1, w).astype(F32)
        dg = acc / cnt - xn[..., ch]
        dg2 = dg.reshape(-1, grp).astype(BF16)
        outs.append(_dot(dg2, w_ref[g]))
    y = jnp.concatenate(outs, axis=-1)
    return y * s_ref[...]


def _pool_prompt_kernel(x_ref, g_ref, w_ref, s_ref, o_ref, np_ref, cat_ref):
    i = pl.program_id(0)
    tm = x_ref.shape[0]

    @pl.when(i == 0)
    def _():
        cat_ref[pl.ds(0, POOL_HALO), :] = jnp.zeros((POOL_HALO, cat_ref.shape[1]), F32)

    x = x_ref[...]
    xn = _rms(x, g_ref[...])
    cat_ref[pl.ds(POOL_HALO, tm), :] = xn
    y = _pool_windows(cat_ref, xn, (), tm, i * tm, w_ref, s_ref)
    o_ref[...] = x + y
    np_ref[...] = cat_ref[pl.ds(POOL_HALO + tm - POOL_BUF, POOL_BUF), :]
    cat_ref[pl.ds(0, POOL_HALO), :] = cat_ref[pl.ds(tm, POOL_HALO), :]


def _pool_prompt(x, g, w_pool_bf, s_pool):
    s, d = x.shape
    tm = ROW_TILE
    assert s % tm == 0 and tm >= POOL_HALO
    return pl.pallas_call(
        _pool_prompt_kernel,
        grid=(s // tm,),
        in_specs=[
            pl.BlockSpec((tm, d), lambda i: (i, 0)),
            pl.BlockSpec((1, d), lambda i: (0, 0)),
            pl.BlockSpec(w_pool_bf.shape, lambda i: (0, 0, 0)),
            pl.BlockSpec((1, d), lambda i: (0, 0)),
        ],
        out_specs=[
            pl.BlockSpec((tm, d), lambda i: (i, 0)),
            pl.BlockSpec((POOL_BUF, d), lambda i: (0, 0)),
        ],
        out_shape=[jax.ShapeDtypeStruct((s, d), F32), jax.ShapeDtypeStruct((POOL_BUF, d), F32)],
        scratch_shapes=[pltpu.VMEM((POOL_HALO + tm, d), F32)],
        compiler_params=pltpu.CompilerParams(dimension_semantics=_arb(1)),
        name="pool_prompt",
    )(x, g, w_pool_bf, s_pool)


def _pool_sample_kernel(start_pos, x_ref, buf_ref, g_ref, w_ref, s_ref, o_ref, np_ref, cat_ref):
    bb, sq, d = x_ref.shape
    x = x_ref[...]
    xn = _rms(x, g_ref[...])
    cat_ref[:, pl.ds(POOL_HALO - POOL_BUF, POOL_BUF), :] = buf_ref[...]
    cat_ref[:, pl.ds(POOL_HALO, sq), :] = xn
    y = _pool_windows(cat_ref, xn, (slice(None),), sq, start_pos, w_ref, s_ref)
    o_ref[...] = x + y.reshape(bb, sq, d)
    np_ref[...] = cat_ref[:, pl.ds(POOL_HALO + sq - POOL_BUF, POOL_BUF), :]


def _pool_sample(x, buf, g, w_pool_bf, s_pool, start_pos):
    b, sq, d = x.shape
    bb = 8
    assert b % bb == 0 and sq == V7X_SUBLANES
    return pl.pallas_call(
        functools.partial(_pool_sample_kernel, start_pos),
        grid=(b // bb,),
        in_specs=[
            pl.BlockSpec((bb, sq, d), lambda i: (i, 0, 0)),
            pl.BlockSpec((bb, POOL_BUF, d), lambda i: (i, 0, 0)),
            pl.BlockSpec((1, d), lambda i: (0, 0)),
            pl.BlockSpec(w_pool_bf.shape, lambda i: (0, 0, 0)),
            pl.BlockSpec((1, d), lambda i: (0, 0)),
        ],
        out_specs=[
            pl.BlockSpec((bb, sq, d), lambda i: (i, 0, 0)),
            pl.BlockSpec((bb, POOL_BUF, d), lambda i: (i, 0, 0)),
        ],
        out_shape=[jax.ShapeDtypeStruct((b, sq, d), F32), jax.ShapeDtypeStruct((b, POOL_BUF, d), F32)],
        scratch_shapes=[pltpu.VMEM((bb, POOL_HALO + sq, d), F32)],
        compiler_params=pltpu.CompilerParams(dimension_semantics=_arb(1)),
        name="pool_sample",
    )(x, buf, g, w_pool_bf, s_pool)


def _router_kernel(x_ref, g_ref, wr_ref, br_ref, xn_ref, e_ref, gate_ref):
    xn = _rms(x_ref[...], g_ref[...])
    xn_ref[...] = xn
    logits = jnp.dot(xn, wr_ref[...], preferred_element_type=F32,
                     precision=lax.Precision.HIGHEST) + br_ref[...]
    tm, ne = logits.shape
    lane = lax.broadcasted_iota(I32, (tm, ne), 1)
    kcol = lax.broadcasted_iota(I32, (tm, TOP_K), 1)
    work = logits
    idx_out = jnp.zeros((tm, TOP_K), I32)
    val_out = jnp.zeros((tm, TOP_K), F32)
    for k in range(TOP_K):
        m = jnp.max(work, axis=-1, keepdims=True)
        idx = jnp.min(jnp.where(work == m, lane, ne), axis=-1, keepdims=True)
        idx_out = jnp.where(kcol == k, idx, idx_out)
        val_out = jnp.where(kcol == k, m, val_out)
        work = jnp.where(lane == idx, -jnp.inf, work)
    ex = jnp.exp(val_out - val_out[:, 0:1])
    e_ref[...] = idx_out
    gate_ref[...] = ex / jnp.sum(ex, axis=-1, keepdims=True)


def _router(x, g, w_router, b_router):
    t, d = x.shape
    tm = ROW_TILE
    ne = w_router.shape[1]
    return pl.pallas_call(
        _router_kernel,
        grid=(t // tm,),
        in_specs=[
            pl.BlockSpec((tm, d), lambda i: (i, 0)),
            pl.BlockSpec((1, d), lambda i: (0, 0)),
            pl.BlockSpec((d, ne), lambda i: (0, 0)),
            pl.BlockSpec((1, ne), lambda i: (0, 0)),
        ],
        out_specs=[
            pl.BlockSpec((tm, d), lambda i: (i, 0)),
            pl.BlockSpec((tm, TOP_K), lambda i: (i, 0)),
            pl.BlockSpec((tm, TOP_K), lambda i: (i, 0)),
        ],
        out_shape=[
            jax.ShapeDtypeStruct((t, d), F32),
            jax.ShapeDtypeStruct((t, TOP_K), I32),
            jax.ShapeDtypeStruct((t, TOP_K), F32),
        ],
        compiler_params=pltpu.CompilerParams(dimension_semantics=_arb(1)),
        name="router",
    )(x, g, w_router, b_router)


def _moe_tables(e4, t, nb_max, nf):
    tm, rb = MOE_TM, MOE_RB
    ne = N_EXPERTS
    sel = (e4[:, :, None] == jnp.arange(ne, dtype=I32)[None, None, :]).any(axis=1)
    seli = sel.astype(I32)
    counts = seli.sum(axis=0)
    nb = (counts + tm - 1) // tm
    bend = jnp.cumsum(nb)
    bstart = bend - nb
    rank = jnp.cumsum(seli, axis=0) - seli
    dest = bstart[None, :] * tm + rank
    pos4 = jnp.take_along_axis(dest, e4, axis=1).astype(I32)
    n_rows = nb_max * tm
    tok = jnp.broadcast_to(jnp.arange(t, dtype=I32)[:, None], (t, TOP_K))
    row_tok = jnp.zeros((n_rows,), I32).at[pos4.reshape(-1)].set(tok.reshape(-1), unique_indices=True)
    ns = nb_max * nf
    steps_e = nf * nb
    send = jnp.cumsum(steps_e)
    sstart = send - steps_e
    total = send[-1]
    s = jnp.arange(ns, dtype=I32)
    sc = jnp.minimum(s, total - 1)
    e = jnp.minimum((send[None, :] <= sc[:, None]).astype(I32).sum(axis=1), ne - 1)
    local = sc - sstart[e]
    per_group = nf * rb
    gi = local // per_group
    rem = local - gi * per_group
    gnb = jnp.minimum(rb, nb[e] - gi * rb)
    j = rem // gnb
    r = rem - j * gnb
    gfirst = bstart[e] + gi * rb
    blk = gfirst + r
    oblk = jnp.where(j == nf - 1, blk, gfirst)
    n_used = bend[-1]
    spare = s - total
    fill = jnp.logical_and(spare >= 0, spare < nb_max - n_used)
    oblk = jnp.where(s < total, oblk, jnp.minimum(n_used + jnp.maximum(spare, 0), nb_max - 1))
    nxt = sc + gnb
    has_next = jnp.logical_and(nxt < total, r == 0)
    nxt = jnp.minimum(nxt, ns - 1)
    flag = ((s < total).astype(I32) + 2 * (r == 0).astype(I32) + 4 * fill.astype(I32)
            + 8 * has_next.astype(I32))
    nbu = bend[-1:].astype(I32)
    return dict(pos4=pos4, row_tok=row_tok, nbu=nbu,
                st_e=e.astype(I32), st_j=j.astype(I32), st_blk=blk.astype(I32),
                st_oblk=oblk.astype(I32), st_r=r.astype(I32), st_flag=flag,
                st_ne=e[nxt].astype(I32), st_nj=j[nxt].astype(I32))


def _row_copy(src_hbm, dst, src_row, dst_row, sem):
    return pltpu.make_async_copy(src_hbm.at[pl.ds(src_row, 1)], dst.at[pl.ds(dst_row, 1)], sem)


def _gather_kernel(nbu_ref, tok_all, x_hbm, o_ref, tok_smem, gbuf, sem_idx, sem):
    i = pl.program_id(0)
    tm = o_ref.shape[0]
    nbu = nbu_ref[0]
    slot = lax.rem(i, 2)

    def issue(blk, sl):
        cp = pltpu.make_async_copy(tok_all.at[blk], tok_smem.at[sl], sem_idx)
        cp.start()
        cp.wait()

        def body(r, c):
            _row_copy(x_hbm, gbuf.at[sl], tok_smem[sl, 0, r], r, sem.at[sl]).start()
            return c
        lax.fori_loop(0, tm, body, 0, unroll=8)

    @pl.when(jnp.logical_and(i == 0, nbu > 0))
    def _():
        issue(0, 0)

    @pl.when(i + 1 < nbu)
    def _():
        issue(i + 1, 1 - slot)

    @pl.when(i < nbu)
    def _():
        pltpu.make_async_copy(x_hbm.at[pl.ds(0, tm)], gbuf.at[slot], sem.at[slot]).wait()
        o_ref[...] = gbuf[slot].astype(BF16)

    @pl.when(i >= nbu)
    def _():
        o_ref[...] = jnp.zeros(o_ref.shape, o_ref.dtype)


def _dispatch_gather(xn, row_tok, nbu, nb_max):
    t, d = xn.shape
    tm = MOE_TM
    tok3 = row_tok.reshape(nb_max, 1, tm)
    return pl.pallas_call(
        _gather_kernel,
        grid_spec=pltpu.PrefetchScalarGridSpec(
            num_scalar_prefetch=1,
            grid=(nb_max,),
            in_specs=[
                pl.BlockSpec((nb_max, 1, tm), lambda i, nbu: (0, 0, 0)),
                pl.BlockSpec(memory_space=pl.ANY),
            ],
            out_specs=pl.BlockSpec((tm, d), lambda i, nbu: (i, 0)),
            scratch_shapes=[
                pltpu.SMEM((2, 1, tm), I32),
                pltpu.VMEM((2, tm, d), F32),
                pltpu.SemaphoreType.DMA(()), pltpu.SemaphoreType.DMA((2,)),
            ],
        ),
        out_shape=jax.ShapeDtypeStruct((nb_max * tm, d), BF16),
        compiler_params=pltpu.CompilerParams(dimension_semantics=_arb(1)),
        name="moe_dispatch",
    )(nbu, tok3, xn)


def _expert_kernel(nf, layer, st_e, st_j, st_blk, st_oblk, st_r, st_flag, st_ne, st_nj,
                   x_ref, bg_ref, bl_ref, bd_ref, wup_hbm, wdn_hbm, o_ref,
                   stg_g, stg_l, stg_d, wg_s, wl_s, wd_s, acc_s, sem):
    s = pl.program_id(0)
    flag = st_flag[s]
    j = st_j[s]
    r = st_r[s]
    tf = stg_g.shape[1]

    def tile_copies(e, jj):
        cg = pl.ds(pl.multiple_of(jj * tf, tf), tf)
        cl = pl.ds(pl.multiple_of((nf + jj) * tf, tf), tf)
        return (pltpu.make_async_copy(wup_hbm.at[layer, e, :, cg], stg_g, sem.at[0]),
                pltpu.make_async_copy(wup_hbm.at[layer, e, :, cl], stg_l, sem.at[1]),
                pltpu.make_async_copy(wdn_hbm.at[layer, e, cg, :], stg_d, sem.at[2]))

    @pl.when(s == 0)
    def _():
        for cp in tile_copies(st_e[0], st_j[0]):
            cp.start()

    @pl.when((flag & 4) != 0)
    def _():
        o_ref[...] = jnp.zeros(o_ref.shape, o_ref.dtype)

    @pl.when((flag & 1) != 0)
    def _():
        @pl.when((flag & 2) != 0)
        def _():
            cg, cl, cd = tile_copies(st_e[s], j)
            cg.wait()
            wg_s[...] = stg_g[...].astype(BF16)
            cl.wait()
            wl_s[...] = stg_l[...].astype(BF16)
            cd.wait()
            wd_s[...] = stg_d[...].astype(BF16)

            @pl.when((flag & 8) != 0)
            def _():
                for cp in tile_copies(st_ne[s], st_nj[s]):
                    cp.start()

        x = x_ref[...]
        hg = _dot(x, wg_s[...]) + bg_ref[...]
        hl = _dot(x, wl_s[...]) + bl_ref[...]
        hg = jnp.minimum(hg, SWIGLU_LIMIT)
        hl = jnp.clip(hl, -SWIGLU_LIMIT, SWIGLU_LIMIT)
        a = hg * jax.nn.sigmoid(SWIGLU_ALPHA * hg) * (hl + 1.0)
        c = _dot(a.astype(BF16), wd_s[...])

        if nf == 1:
            o_ref[...] = c + bd_ref[...]
        else:
            @pl.when(j == 0)
            def _():
                acc_s[r] = c + bd_ref[...]

            @pl.when(jnp.logical_and(j > 0, j < nf - 1))
            def _():
                acc_s[r] += c

            @pl.when(j == nf - 1)
            def _():
                o_ref[...] = acc_s[r] + c


def _expert_mlp(x_sorted, tabs, layer, w_up, b_up, w_down, b_down, nb_max):
    n_rows, d = x_sorted.shape
    ne, f2 = b_up.shape
    f = f2 // 2
    tm, tf, rb = MOE_TM, MOE_TF, MOE_RB
    nf = f // tf
    ns = nb_max * nf
    b_up3 = b_up.reshape(ne, 1, f2)
    b_down3 = b_down.reshape(ne, 1, d)
    x_bytes = jnp.dtype(x_sorted.dtype).itemsize
    vmem = (3 * d * tf * 4) + (3 * d * tf * 2) + rb * tm * d * 4 + 2 * tm * d * x_bytes + 2 * tm * d * 4 + (6 << 20)
    assert vmem <= V7X_VMEM_LIMIT, vmem
    idx = lambda f_: (lambda s, e, j, b, ob, r, fl, ne_, nj_: f_(s, e, j, b, ob))
    return pl.pallas_call(
        functools.partial(_expert_kernel, nf, layer),
        grid_spec=pltpu.PrefetchScalarGridSpec(
            num_scalar_prefetch=8,
            grid=(ns,),
            in_specs=[
                pl.BlockSpec((tm, d), idx(lambda s, e, j, b, ob: (b[s], 0))),
                pl.BlockSpec((None, 1, tf), idx(lambda s, e, j, b, ob: (e[s], 0, j[s]))),
                pl.BlockSpec((None, 1, tf), idx(lambda s, e, j, b, ob: (e[s], 0, nf + j[s]))),
                pl.BlockSpec((None, 1, d), idx(lambda s, e, j, b, ob: (e[s], 0, 0))),
                pl.BlockSpec(memory_space=pl.ANY),
                pl.BlockSpec(memory_space=pl.ANY),
            ],
            out_specs=pl.BlockSpec((tm, d), idx(lambda s, e, j, b, ob: (ob[s], 0))),
            scratch_shapes=[
                pltpu.VMEM((d, tf), F32), pltpu.VMEM((d, tf), F32), pltpu.VMEM((tf, d), F32),
                pltpu.VMEM((d, tf), BF16), pltpu.VMEM((d, tf), BF16), pltpu.VMEM((tf, d), BF16),
                pltpu.VMEM((rb, tm, d), F32),
                pltpu.SemaphoreType.DMA((3,)),
            ],
        ),
        out_shape=jax.ShapeDtypeStruct((n_rows, d), F32),
        compiler_params=pltpu.CompilerParams(dimension_semantics=_arb(1), vmem_limit_bytes=vmem),
        name="moe_experts",
    )(tabs['st_e'], tabs['st_j'], tabs['st_blk'], tabs['st_oblk'], tabs['st_r'], tabs['st_flag'],
      tabs['st_ne'], tabs['st_nj'],
      x_sorted, b_up3, b_up3, b_down3, w_up, w_down)


def _combine_kernel(final_norm, pos_vmem, g_ref, x_ref, gf_ref, y_hbm, o_ref, pos_smem, ybuf, sem_idx, sem):
    tt = x_ref.shape[0]
    cp = pltpu.make_async_copy(pos_vmem.at[0], pos_smem, sem_idx)
    cp.start()
    cp.wait()
    for k in range(TOP_K):
        def issue(r, c, k=k):
            _row_copy(y_hbm, ybuf.at[k], pos_smem[0, k * tt + r], r, sem).start()
            return c
        lax.fori_loop(0, tt, issue, 0, unroll=8)
    for k in range(TOP_K):
        pltpu.make_async_copy(y_hbm.at[pl.ds(0, tt)], ybuf.at[k], sem).wait()
    g = g_ref[...]
    acc = x_ref[...]
    for k in range(TOP_K):
        acc = acc + g[:, k:k + 1] * ybuf[k]
    if final_norm:
        acc = _rms(acc, gf_ref[...])
    o_ref[...] = acc


def _combine(x, y_sorted, pos4, gates, g_final, final_norm):
    t, d = x.shape
    tt = COMBINE_TT
    nt = t // tt
    pos3 = pos4.reshape(nt, tt, TOP_K).transpose(0, 2, 1).reshape(nt, 1, TOP_K * tt)
    return pl.pallas_call(
        functools.partial(_combine_kernel, final_norm),
        grid=(nt,),
        in_specs=[
            pl.BlockSpec((1, 1, TOP_K * tt), lambda i: (i, 0, 0)),
            pl.BlockSpec((tt, TOP_K), lambda i: (i, 0)),
            pl.BlockSpec((tt, d), lambda i: (i, 0)),
            pl.BlockSpec((1, d), lambda i: (0, 0)),
            pl.BlockSpec(memory_space=pl.ANY),
        ],
        out_specs=pl.BlockSpec((tt, d), lambda i: (i, 0)),
        out_shape=jax.ShapeDtypeStruct((t, d), F32),
        scratch_shapes=[
            pltpu.SMEM((1, TOP_K * tt), I32),
            pltpu.VMEM((TOP_K, tt, d), F32),
            pltpu.SemaphoreType.DMA(()), pltpu.SemaphoreType.DMA(()),
        ],
        compiler_params=pltpu.CompilerParams(dimension_semantics=_arb(1)),
        name="moe_combine",
    )(pos3, gates, x, g_final, y_sorted)


def _moe_layer(x, layer, g_ffn, w_router, b_router, w_up, b_up, w_down, b_down, g_final, final_norm):
    t, d = x.shape
    f = w_up.shape[3] // 2
    nf = f // MOE_TF
    nb_max = -(-(t * TOP_K) // MOE_TM) + N_EXPERTS
    xn, e4, gates = _router(x, g_ffn, w_router, b_router)
    tabs = _moe_tables(e4, t, nb_max, nf)
    x_sorted = _dispatch_gather(xn, tabs['row_tok'], tabs['nbu'], nb_max)
    y_sorted = _expert_mlp(x_sorted, tabs, layer, w_up, b_up, w_down, b_down, nb_max)
    return _combine(x, y_sorted, tabs['pos4'], gates, g_final, final_norm)


def _proj_kernel(x_ref, cos_ref, sin_ref, cost_ref, sint_ref, gkv_ref, gmix_ref, gckv_ref, gq_ref,
                 wc_ref, wp_ref, wpr_ref, wdq_ref, wqn_ref, wqp_ref, wqpr_ref, wqnt_ref, wqpt_ref, wqprt_ref,
                 ckv_ref, kpe_ref, q_ref, qt_ref):
    x = x_ref[...]
    cos = cos_ref[...]
    sin = sin_ref[...]
    xkv = _rms(x, gkv_ref[...]).astype(BF16)
    ckv_ref[...] = _rms(_dot(xkv, wc_ref[...]), gckv_ref[...])
    kpe_ref[...] = _dot(xkv, wp_ref[...]) * cos + _dot(xkv, wpr_ref[...]) * sin
    xq = _rms(x, gmix_ref[...]).astype(BF16)
    cq = _rms(_dot(xq, wdq_ref[...]), gq_ref[...]).astype(BF16)
    qn = _dot(cq, wqn_ref[...]) * ATTN_SCALE
    qp = _dot(cq, wqp_ref[...])
    qpr = _dot(cq, wqpr_ref[...])
    for h in range(N_HEADS):
        q_ref[h, :, 0:D_NOPE] = qn[:, h * D_NOPE:(h + 1) * D_NOPE].astype(BF16)
        ph = qp[:, h * D_ROPE:(h + 1) * D_ROPE] * cos + qpr[:, h * D_ROPE:(h + 1) * D_ROPE] * sin
        q_ref[h, :, D_NOPE:D_QK] = (ph * ATTN_SCALE).astype(BF16)
    cost = cost_ref[...]
    sint = sint_ref[...]
    qnt = _dot_nt(wqnt_ref[...], cq) * ATTN_SCALE
    qpt = _dot_nt(wqpt_ref[...], cq)
    qprt = _dot_nt(wqprt_ref[...], cq)
    for h in range(N_HEADS):
        qt_ref[h, 0:D_NOPE, :] = qnt[h * D_NOPE:(h + 1) * D_NOPE, :].astype(BF16)
        pht = qpt[h * D_ROPE:(h + 1) * D_ROPE, :] * cost + qprt[h * D_ROPE:(h + 1) * D_ROPE, :] * sint
        qt_ref[h, D_NOPE:D_QK, :] = (pht * ATTN_SCALE).astype(BF16)


def _proj(x, cos, sin, g_kv, g_mix, g_ckv, g_q, wc, wp, wpr, wdq, wqn, wqp, wqpr):
    t, d = x.shape
    tm = ROW_TILE
    kvl = wc.shape[1]
    full = lambda a: pl.BlockSpec(a.shape, lambda i: (0,) * a.ndim)
    row = lambda n: pl.BlockSpec((tm, n), lambda i: (i, 0))
    col = lambda n: pl.BlockSpec((n, tm), lambda i: (0, i))
    consts = (g_kv, g_mix, g_ckv, g_q, wc, wp, wpr, wdq, wqn, wqp, wqpr, wqn.T, wqp.T, wqpr.T)
    return pl.pallas_call(
        _proj_kernel,
        grid=(t // tm,),
        in_specs=[row(d), row(D_ROPE), row(D_ROPE), col(D_ROPE), col(D_ROPE)] + [full(a) for a in consts],
        out_specs=[row(kvl), row(D_ROPE), pl.BlockSpec((N_HEADS, tm, D_QK), lambda i: (0, i, 0)),
                   pl.BlockSpec((N_HEADS, D_QK, tm), lambda i: (0, 0, i))],
        out_shape=[
            jax.ShapeDtypeStruct((t, kvl), F32),
            jax.ShapeDtypeStruct((t, D_ROPE), F32),
            jax.ShapeDtypeStruct((N_HEADS, t, D_QK), BF16),
            jax.ShapeDtypeStruct((N_HEADS, D_QK, t), BF16),
        ],
        compiler_params=pltpu.CompilerParams(dimension_semantics=_arb(1), vmem_limit_bytes=48 << 20),
        name="latent_q_proj",
    )(x, cos, sin, cos.T, sin.T, *consts)


def _kv_up_kernel(ckv_ref, kpe_ref, wuk_ref, wuv_ref, k_ref, v_ref):
    c = ckv_ref[...].astype(BF16)
    kn = _dot(c, wuk_ref[...])
    vv = _dot(c, wuv_ref[...])
    kp = kpe_ref[...].astype(BF16)
    for h in range(N_HEADS):
        k_ref[h, :, 0:D_NOPE] = kn[:, h * D_NOPE:(h + 1) * D_NOPE].astype(BF16)
        k_ref[h, :, D_NOPE:D_QK] = kp
        v_ref[h] = vv[:, h * D_V:(h + 1) * D_V].astype(BF16)


def _kv_up(ckv, kpe, s, wuk2, wuv2):
    tm = ROW_TILE
    kvl = ckv.shape[1]
    return pl.pallas_call(
        _kv_up_kernel,
        grid=(s // tm,),
        in_specs=[
            pl.BlockSpec((tm, kvl), lambda i: (i, 0)),
            pl.BlockSpec((tm, D_ROPE), lambda i: (i, 0)),
            pl.BlockSpec(wuk2.shape, lambda i: (0, 0)),
            pl.BlockSpec(wuv2.shape, lambda i: (0, 0)),
        ],
        out_specs=[
            pl.BlockSpec((N_HEADS, tm, D_QK), lambda i: (0, i, 0)),
            pl.BlockSpec((N_HEADS, tm, D_V), lambda i: (0, i, 0)),
        ],
        out_shape=[
            jax.ShapeDtypeStruct((N_HEADS, s, D_QK), BF16),
            jax.ShapeDtypeStruct((N_HEADS, s, D_V), BF16),
        ],
        compiler_params=pltpu.CompilerParams(dimension_semantics=_arb(1)),
        name="kv_up",
    )(ckv, kpe, wuk2, wuv2)


def _flash_kernel(ti_ref, tj_ref, qt_ref, k_ref, v_ref, o_ref, m_s, l_s, acc_s):
    n = pl.program_id(1)
    i = ti_ref[n]
    j = tj_ref[n]
    ng, _, tq = qt_ref.shape
    tk = k_ref.shape[1]

    @pl.when(j == 0)
    def _():
        m_s[...] = jnp.full(m_s.shape, -jnp.inf, F32)
        l_s[...] = jnp.zeros(l_s.shape, F32)
        acc_s[...] = jnp.zeros(acc_s.shape, F32)

    def chain(u, masked):
        st = _dot(k_ref[u], qt_ref[u])
        if masked:
            key = lax.broadcasted_iota(I32, (tk, tq), 0)
            qry = lax.broadcasted_iota(I32, (tk, tq), 1)
            st = jnp.where(key <= qry, st, -jnp.inf)
        m_prev = m_s[u]
        m_new = jnp.maximum(m_prev, jnp.max(st, axis=0, keepdims=True))
        corr = jnp.exp(m_prev - m_new)
        pt = jnp.exp(st - m_new)
        l_s[u] = l_s[u] * corr + jnp.sum(pt, axis=0, keepdims=True)
        acc_s[u] = acc_s[u] * corr + _dot_tn(v_ref[u], pt.astype(BF16))
        m_s[u] = m_new

    @pl.when(j < i)
    def _():
        for u in range(ng):
            chain(u, False)

    @pl.when(j == i)
    def _():
        for u in range(ng):
            chain(u, True)
            o_ref[:, u * D_V:(u + 1) * D_V] = (acc_s[u] / l_s[u]).T.astype(o_ref.dtype)


def _flash_prompt(q_t, k_cat, v, s):
    tq = ATT_TQ
    ng = ATT_HEAD_GROUP
    assert tq == ATT_TK and s % tq == 0 and N_HEADS % ng == 0
    nq = s // tq
    pairs = [(i, j) for i in range(nq) for j in range(i + 1)]
    ti = jnp.asarray([p[0] for p in pairs], I32)
    tj = jnp.asarray([p[1] for p in pairs], I32)
    return pl.pallas_call(
        _flash_kernel,
        grid_spec=pltpu.PrefetchScalarGridSpec(
            num_scalar_prefetch=2,
            grid=(N_HEADS // ng, len(pairs)),
            in_specs=[
                pl.BlockSpec((ng, D_QK, tq), lambda g, n, ti, tj: (g, 0, ti[n])),
                pl.BlockSpec((ng, tq, D_QK), lambda g, n, ti, tj: (g, tj[n], 0)),
                pl.BlockSpec((ng, tq, D_V), lambda g, n, ti, tj: (g, tj[n], 0)),
            ],
            out_specs=pl.BlockSpec((tq, ng * D_V), lambda g, n, ti, tj: (ti[n], g)),
            scratch_shapes=[
                pltpu.VMEM((ng, 1, tq), F32), pltpu.VMEM((ng, 1, tq), F32),
                pltpu.VMEM((ng, D_V, tq), F32),
            ],
        ),
        out_shape=jax.ShapeDtypeStruct((s, N_HEADS * D_V), BF16),
        compiler_params=pltpu.CompilerParams(dimension_semantics=_arb(2), vmem_limit_bytes=48 << 20),
        name="flash_prompt",
    )(ti, tj, q_t, k_cat, v)


def _q_absorb_kernel(q_ref, wukt_ref, qlat_ref, qpe_ref):
    q = q_ref[...]
    qlat_ref[...] = _dot(q[:, 0:D_NOPE], wukt_ref[...])
    qpe_ref[...] = q[:, D_NOPE:D_QK].astype(F32)


def _q_absorb(q_cat, row0, ts, wukt):
    kvl = wukt.shape[2]
    assert row0 % ts == 0
    rb = row0 // ts
    return pl.pallas_call(
        _q_absorb_kernel,
        grid=(N_HEADS,),
        in_specs=[
            pl.BlockSpec((None, ts, D_QK), lambda h: (h, rb, 0)),
            pl.BlockSpec((None, D_NOPE, kvl), lambda h: (h, 0, 0)),
        ],
        out_specs=[
            pl.BlockSpec((None, ts, kvl), lambda h: (h, 0, 0)),
            pl.BlockSpec((None, ts, D_ROPE), lambda h: (h, 0, 0)),
        ],
        out_shape=[
            jax.ShapeDtypeStruct((N_HEADS, ts, kvl), F32),
            jax.ShapeDtypeStruct((N_HEADS, ts, D_ROPE), F32),
        ],
        compiler_params=pltpu.CompilerParams(dimension_semantics=_arb(1)),
        name="q_absorb",
    )(q_cat, wukt)


def _dec_attn_kernel(n_chunks, pt_ref, qlat_ref, qpe_ref, cnew_ref, pnew_ref, ckv_hbm, kpe_hbm, o_ref,
                     cbuf, pbuf, sem, m_s, l_s, acc_s, qlt_s, qpt_s, cn_s, pn_s, st_s):
    b = pl.program_id(0)
    c = pl.program_id(1)
    nb = pl.num_programs(0)
    n = b * n_chunks + c
    slot = lax.rem(n, 2)
    npg = DEC_PAGES
    nh, sq, kvl = qlat_ref.shape
    rows = nh * sq

    def fetch(step, slot_):
        bb = step // n_chunks
        cc = step - bb * n_chunks

        def body(i, carry):
            page = pt_ref[bb, cc * npg + i]
            pltpu.make_async_copy(ckv_hbm.at[page], cbuf.at[slot_, i], sem.at[0, slot_]).start()
            pltpu.make_async_copy(kpe_hbm.at[page], pbuf.at[slot_, i], sem.at[1, slot_]).start()
            return carry
        lax.fori_loop(0, npg, body, 0, unroll=8)

    @pl.when(n == 0)
    def _():
        fetch(n, slot)

    @pl.when(n + 1 < nb * n_chunks)
    def _():
        fetch(n + 1, 1 - slot)

    def to_col(v):
        eye = (lax.broadcasted_iota(I32, (rows, rows), 0) == lax.broadcasted_iota(I32, (rows, rows), 1))
        return jnp.sum(jnp.where(eye, jnp.broadcast_to(v, (rows, rows)), 0.0), axis=1, keepdims=True)

    @pl.when(c == 0)
    def _():
        qlt_s[...] = qlat_ref[...].reshape(rows, kvl).T.astype(BF16)
        qpt_s[...] = qpe_ref[...].reshape(rows, D_ROPE).T.astype(BF16)
        cn_s[...] = jnp.zeros(cn_s.shape, BF16)
        pn_s[...] = jnp.zeros(pn_s.shape, BF16)
        cn_s[0:sq, :] = cnew_ref[...].astype(BF16)
        pn_s[0:sq, :] = pnew_ref[...].astype(BF16)
        cn = cn_s[...]
        st = _dot(cn, qlt_s[...]) + _dot(pn_s[...], qpt_s[...])
        tkey = lax.broadcasted_iota(I32, st.shape, 0)
        srow = lax.rem(lax.broadcasted_iota(I32, st.shape, 1), sq)
        st = jnp.where(tkey <= srow, st, -jnp.inf)
        m = jnp.max(st, axis=0, keepdims=True)
        pt = jnp.exp(st - m)
        m_s[...] = m
        l_s[...] = jnp.sum(pt, axis=0, keepdims=True)
        acc_s[...] = _dot_tn(pt.astype(BF16), cn)

    pltpu.make_async_copy(ckv_hbm.at[pl.ds(0, npg)], cbuf.at[slot], sem.at[0, slot]).wait()
    pltpu.make_async_copy(kpe_hbm.at[pl.ds(0, npg)], pbuf.at[slot], sem.at[1, slot]).wait()

    sub = DEC_SUB_PAGES
    nsub = npg // sub
    qlt = qlt_s[...]
    qpt = qpt_s[...]

    def sub_block(u):
        ck = cbuf[slot, u * sub:(u + 1) * sub].reshape(sub * PAGE_SIZE, kvl).astype(BF16)
        kpt = jnp.concatenate([pbuf[slot, u * sub + i] for i in range(sub)], axis=1).astype(BF16)
        return ck, kpt

    m_prev = m_s[...]
    m_new = m_prev
    for u in range(nsub):
        ck, kpt = sub_block(u)
        st = _dot(ck, qlt) + _dot_tn(kpt, qpt)
        st_s[u] = st
        m_new = jnp.maximum(m_new, jnp.max(st, axis=0, keepdims=True))
    l_step = jnp.zeros_like(m_new)
    acc_step = jnp.zeros(acc_s.shape, F32)
    for u in range(nsub):
        ck, _ = sub_block(u)
        pt = jnp.exp(st_s[u] - m_new)
        l_step = l_step + jnp.sum(pt, axis=0, keepdims=True)
        acc_step = acc_step + _dot_tn(pt.astype(BF16), ck)
    corr = jnp.exp(m_prev - m_new)
    l_new = l_s[...] * corr + l_step
    acc_new = acc_s[...] * to_col(corr) + acc_step
    m_s[...] = m_new
    l_s[...] = l_new
    acc_s[...] = acc_new

    @pl.when(c == n_chunks - 1)
    def _():
        o_ref[...] = (acc_new / to_col(l_new)).reshape(nh, sq, kvl)


def _dec_attn(q_lat, q_pe, ckv_new, kpe_new, cache_ckv, cache_kpe_t, page_table):
    nh, ts, kvl = q_lat.shape
    bsz, n_pages = page_table.shape
    sq = ts // bsz
    assert n_pages % DEC_PAGES == 0 and sq == V7X_SUBLANES
    n_chunks = n_pages // DEC_PAGES
    rows = nh * sq
    assert DEC_PAGES % DEC_SUB_PAGES == 0 and cache_kpe_t.shape[1:] == (D_ROPE, PAGE_SIZE)
    return pl.pallas_call(
        functools.partial(_dec_attn_kernel, n_chunks),
        grid_spec=pltpu.PrefetchScalarGridSpec(
            num_scalar_prefetch=1,
            grid=(bsz, n_chunks),
            in_specs=[
                pl.BlockSpec((nh, sq, kvl), lambda b, c, pt: (0, b, 0)),
                pl.BlockSpec((nh, sq, D_ROPE), lambda b, c, pt: (0, b, 0)),
                pl.BlockSpec((sq, kvl), lambda b, c, pt: (b, 0)),
                pl.BlockSpec((sq, D_ROPE), lambda b, c, pt: (b, 0)),
                pl.BlockSpec(memory_space=pl.ANY),
                pl.BlockSpec(memory_space=pl.ANY),
            ],
            out_specs=pl.BlockSpec((nh, sq, kvl), lambda b, c, pt: (0, b, 0)),
            scratch_shapes=[
                pltpu.VMEM((2, DEC_PAGES, PAGE_SIZE, kvl), F32),
                pltpu.VMEM((2, DEC_PAGES, D_ROPE, PAGE_SIZE), F32),
                pltpu.SemaphoreType.DMA((2, 2)),
                pltpu.VMEM((1, rows), F32), pltpu.VMEM((1, rows), F32), pltpu.VMEM((rows, kvl), F32),
                pltpu.VMEM((kvl, rows), BF16), pltpu.VMEM((D_ROPE, rows), BF16),
                pltpu.VMEM((PAGE_SIZE, kvl), BF16), pltpu.VMEM((PAGE_SIZE, D_ROPE), BF16),
                pltpu.VMEM((DEC_PAGES // DEC_SUB_PAGES, DEC_SUB_PAGES * PAGE_SIZE, rows), F32),
            ],
        ),
        out_shape=jax.ShapeDtypeStruct((nh, ts, kvl), F32),
        compiler_params=pltpu.CompilerParams(dimension_semantics=_arb(2), vmem_limit_bytes=48 << 20),
        name="dec_attn",
    )(page_table, q_lat, q_pe, ckv_new, kpe_new, cache_ckv, cache_kpe_t)


def _v_up_kernel(ol_ref, wuv_ref, o_ref):
    o_ref[...] = _dot(ol_ref[...].astype(BF16), wuv_ref[...]).astype(o_ref.dtype)


def _v_up(o_lat, wuv_h):
    nh, ts, kvl = o_lat.shape
    return pl.pallas_call(
        _v_up_kernel,
        grid=(nh,),
        in_specs=[
            pl.BlockSpec((None, ts, kvl), lambda h: (h, 0, 0)),
            pl.BlockSpec((None, kvl, D_V), lambda h: (h, 0, 0)),
        ],
        out_specs=pl.BlockSpec((ts, D_V), lambda h: (0, h)),
        out_shape=jax.ShapeDtypeStruct((ts, nh * D_V), BF16),
        compiler_params=pltpu.CompilerParams(dimension_semantics=_arb(1)),
        name="v_up",
    )(o_lat, wuv_h)


def _oproj_kernel(o_ref, x_ref, w_ref, y_ref):
    y_ref[...] = x_ref[...] + _dot(o_ref[...], w_ref[...])


def _oproj(o, x, w_o_bf):
    t, d = x.shape
    tm = ROW_TILE
    return pl.pallas_call(
        _oproj_kernel,
        grid=(t // tm,),
        in_specs=[
            pl.BlockSpec((tm, o.shape[1]), lambda i: (i, 0)),
            pl.BlockSpec((tm, d), lambda i: (i, 0)),
            pl.BlockSpec(w_o_bf.shape, lambda i: (0, 0)),
        ],
        out_specs=pl.BlockSpec((tm, d), lambda i: (i, 0)),
        out_shape=jax.ShapeDtypeStruct((t, d), F32),
        compiler_params=pltpu.CompilerParams(dimension_semantics=_arb(1), vmem_limit_bytes=40 << 20),
        name="attn_out_proj",
    )(o, x, w_o_bf)


def _rot_half_cols(w):
    half = D_ROPE // 2
    return jnp.concatenate([-w[..., half:], w[..., :half]], axis=-1)


def _rope_tables(pos):
    half = D_ROPE // 2
    inv_freq = jnp.power(ROPE_BASE, -jnp.arange(half, dtype=F32) / half)
    ang = pos.astype(F32)[:, None] * inv_freq[None, :]
    cos = jnp.cos(ang)
    sin = jnp.sin(ang)
    return jnp.concatenate([cos, cos], axis=-1), jnp.concatenate([sin, sin], axis=-1)


def kernel(x_prompt, x_sample, state_pool, cache_ckv, cache_kpe, page_table, g_mix, g_ffn, w_pool, s_pool,
           g_kv, w_dkv, g_ckv, w_uk, w_uv, w_dq, g_q, w_uq, w_o, w_router, b_router, w_up, b_up,
           w_down, b_down, g_final):
    bp, s, d = x_prompt.shape
    bd, sq, _ = x_sample.shape
    assert bp == 1 and g_mix.shape[0] == 2 and state_pool.shape[0] == 1
    ts = bd * sq
    t = s + ts
    past_len = page_table.shape[1] * cache_ckv.shape[1]
    kvl = w_uk.shape[0]
    row2 = lambda v: v.reshape(1, -1)

    w_pool_bf = w_pool[0].astype(BF16)
    x1p, pool_p = _pool_prompt(x_prompt[0], row2(g_mix[0]), w_pool_bf, row2(s_pool[0]))
    x1s, pool_s = _pool_sample(x_sample, state_pool[0], row2(g_mix[0]), w_pool_bf, row2(s_pool[0]), past_len)
    x1 = jnp.concatenate([x1p, x1s.reshape(ts, d)], axis=0)

    x2 = _moe_layer(x1, 0, row2(g_ffn[0]), w_router[0], row2(b_router[0]), w_up, b_up[0], w_down, b_down[0],
                    row2(g_final), False)

    pos = jnp.concatenate([jnp.arange(s, dtype=I32),
                           jnp.tile(past_len + jnp.arange(sq, dtype=I32), bd)])
    cos, sin = _rope_tables(pos)
    w_dkv_c = w_dkv[:, :kvl].astype(BF16)
    w_dkv_p = w_dkv[:, kvl:]
    w_uq1 = w_uq[0]
    ql = w_uq1.shape[0]
    w_qn = w_uq1[:, :, :D_NOPE].reshape(ql, N_HEADS * D_NOPE).astype(BF16)
    w_qp = w_uq1[:, :, D_NOPE:]
    ckv, kpe, q_cat, q_t = _proj(
        x2, cos, sin, row2(g_kv), row2(g_mix[1]), row2(g_ckv), row2(g_q[0]),
        w_dkv_c, w_dkv_p.astype(BF16), _rot_half_cols(w_dkv_p).astype(BF16), w_dq[0].astype(BF16),
        w_qn, w_qp.reshape(ql, N_HEADS * D_ROPE).astype(BF16),
        _rot_half_cols(w_qp).reshape(ql, N_HEADS * D_ROPE).astype(BF16))

    wuk2 = w_uk.reshape(kvl, N_HEADS * D_NOPE).astype(BF16)
    wuv2 = w_uv.reshape(kvl, N_HEADS * D_V).astype(BF16)
    k_cat, v = _kv_up(ckv, kpe, s, wuk2, wuv2)
    o_p = _flash_prompt(q_t, k_cat, v, s)

    wukt = jnp.transpose(w_uk, (1, 2, 0)).astype(BF16)
    wuv_h = jnp.transpose(w_uv, (1, 0, 2)).astype(BF16)
    q_lat, q_pe = _q_absorb(q_cat, s, ts, wukt)
    ckv_s = ckv[s:]
    kpe_s = kpe[s:]
    o_lat = _dec_attn(q_lat, q_pe, ckv_s, kpe_s, cache_ckv, jnp.swapaxes(cache_kpe, 1, 2), page_table)
    o_s = _v_up(o_lat, wuv_h)

    o = jnp.concatenate([o_p, o_s], axis=0)
    x3 = _oproj(o, x2, w_o[0].astype(BF16))
    y = _moe_layer(x3, 1, row2(g_ffn[1]), w_router[1], row2(b_router[1]), w_up, b_up[1], w_down, b_down[1],
                   row2(g_final), True)

    return (y[:s].reshape(1, s, d), y[s:].reshape(bd, sq, d),
            pool_p.reshape(1, 1, POOL_BUF, d), pool_s.reshape(1, bd, POOL_BUF, d),
            ckv[:s].reshape(1, s, kvl), kpe[:s].reshape(1, s, D_ROPE),
            ckv_s.reshape(bd, sq, kvl), kpe_s.reshape(bd, sq, D_ROPE))
```

```python
import functools

import jax
import jax.numpy as jnp
from jax import lax
from jax.experimental import pallas as pl
from jax.experimental.pallas import tpu as pltpu

POOL_WINDOWS = (2, 4, 8, 16)
POOL_BUF = max(POOL_WINDOWS) - 1
N_HEADS = 16
D_NOPE = 128
D_ROPE = 64
D_QK = D_NOPE + D_ROPE
D_V = 128
ROPE_BASE = 10000.0
ATTN_SCALE = D_QK ** -0.5
LOG2_E = 1.4426950408889634
N_EXPERTS = 32
TOP_K = 4
SWIGLU_ALPHA = 1.702
SWIGLU_LIMIT = 7.0
EPS = 1e-6
PAGE_SIZE = 128

V7X_SUBLANES = 8
V7X_LANES = 128
V7X_VMEM_BYTES = 64 * 1024 * 1024
V7X_VMEM_LIMIT = 56 * 1024 * 1024

ROW_TILE = 256
POOL_HALO = 2 * V7X_SUBLANES
MOE_TM = 256
MOE_TF = 512
MOE_RB = 8
COMBINE_TT = 128
ATT_TQ = 512
ATT_TK = 512
ATT_HEAD_GROUP = 8
DEC_PAGES = 64
DEC_SUB_PAGES = 4

BF16 = jnp.bfloat16
F32 = jnp.float32
I32 = jnp.int32


def _arb(n):
    return ("arbitrary",) * n


def _rms(x, g):
    return x * lax.rsqrt(jnp.mean(x * x, axis=-1, keepdims=True) + EPS) * g


def _dot(a, b):
    return jnp.dot(a, b, preferred_element_type=F32)


def _dot_nt(a, b):
    return lax.dot_general(a, b, (((1,), (1,)), ((), ())), preferred_element_type=F32)


def _dot_tn(a, b):
    return lax.dot_general(a, b, (((0,), (0,)), ((), ())), preferred_element_type=F32)


def _pool_windows(cat_ref, xn, lead, rows, start_pos, w_ref, s_ref):
    d = xn.shape[-1]
    grp = d // len(POOL_WINDOWS)
    t_axis = xn.ndim - 2
    pos = start_pos + lax.broadcasted_iota(I32, xn.shape[:-1] + (1,), t_axis)
    outs = []
    for g, w in enumerate(POOL_WINDOWS):
        ch = slice(g * grp, (g + 1) * grp)
        acc = None
        for k in range(w):
            piece = cat_ref[lead + (pl.ds(POOL_HALO - k, rows), ch)]
            acc = piece if acc is None else acc + piece
        cnt = jnp.minimum(pos + 1, w).astype(F32)
        dg = acc / cnt - xn[..., ch]
        dg2 = dg.reshape(-1, grp).astype(BF16)
        outs.append(_dot(dg2, w_ref[g]))
    y = jnp.concatenate(outs, axis=-1)
    return y * s_ref[...]


def _pool_prompt_kernel(x_ref, g_ref, w_ref, s_ref, o_ref, np_ref, cat_ref):
    i = pl.program_id(0)
    tm = x_ref.shape[0]

    @pl.when(i == 0)
    def _():
        cat_ref[pl.ds(0, POOL_HALO), :] = jnp.zeros((POOL_HALO, cat_ref.shape[1]), F32)

    x = x_ref[...]
    xn = _rms(x, g_ref[...])
    cat_ref[pl.ds(POOL_HALO, tm), :] = xn
    y = _pool_windows(cat_ref, xn, (), tm, i * tm, w_ref, s_ref)
    o_ref[...] = x + y
    np_ref[...] = cat_ref[pl.ds(POOL_HALO + tm - POOL_BUF, POOL_BUF), :]
    cat_ref[pl.ds(0, POOL_HALO), :] = cat_ref[pl.ds(tm, POOL_HALO), :]


def _pool_prompt(x, g, w_pool_bf, s_pool):
    s, d = x.shape
    tm = ROW_TILE
    assert s % tm == 0 and tm >= POOL_HALO
    return pl.pallas_call(
        _pool_prompt_kernel,
        grid=(s // tm,),
        in_specs=[
            pl.BlockSpec((tm, d), lambda i: (i, 0)),
            pl.BlockSpec((1, d), lambda i: (0, 0)),
            pl.BlockSpec(w_pool_bf.shape, lambda i: (0, 0, 0)),
            pl.BlockSpec((1, d), lambda i: (0, 0)),
        ],
        out_specs=[
            pl.BlockSpec((tm, d), lambda i: (i, 0)),
            pl.BlockSpec((POOL_BUF, d), lambda i: (0, 0)),
        ],
        out_shape=[jax.ShapeDtypeStruct((s, d), F32), jax.ShapeDtypeStruct((POOL_BUF, d), F32)],
        scratch_shapes=[pltpu.VMEM((POOL_HALO + tm, d), F32)],
        compiler_params=pltpu.CompilerParams(dimension_semantics=_arb(1)),
        name="pool_prompt",
    )(x, g, w_pool_bf, s_pool)


def _pool_sample_kernel(start_pos, x_ref, buf_ref, g_ref, w_ref, s_ref, o_ref, np_ref, cat_ref):
    bb, sq, d = x_ref.shape
    x = x_ref[...]
    xn = _rms(x, g_ref[...])
    cat_ref[:, pl.ds(POOL_HALO - POOL_BUF, POOL_BUF), :] = buf_ref[...]
    cat_ref[:, pl.ds(POOL_HALO, sq), :] = xn
    y = _pool_windows(cat_ref, xn, (slice(None),), sq, start_pos, w_ref, s_ref)
    o_ref[...] = x + y.reshape(bb, sq, d)
    np_ref[...] = cat_ref[:, pl.ds(POOL_HALO + sq - POOL_BUF, POOL_BUF), :]


def _pool_sample(x, buf, g, w_pool_bf, s_pool, start_pos):
    b, sq, d = x.shape
    bb = 8
    assert b % bb == 0 and sq == V7X_SUBLANES
    return pl.pallas_call(
        functools.partial(_pool_sample_kernel, start_pos),
        grid=(b // bb,),
        in_specs=[
            pl.BlockSpec((bb, sq, d), lambda i: (i, 0, 0)),
            pl.BlockSpec((bb, POOL_BUF, d), lambda i: (i, 0, 0)),
            pl.BlockSpec((1, d), lambda i: (0, 0)),
            pl.BlockSpec(w_pool_bf.shape, lambda i: (0, 0, 0)),
            pl.BlockSpec((1, d), lambda i: (0, 0)),
        ],
        out_specs=[
            pl.BlockSpec((bb, sq, d), lambda i: (i, 0, 0)),
            pl.BlockSpec((bb, POOL_BUF, d), lambda i: (i, 0, 0)),
        ],
        out_shape=[jax.ShapeDtypeStruct((b, sq, d), F32), jax.ShapeDtypeStruct((b, POOL_BUF, d), F32)],
        scratch_shapes=[pltpu.VMEM((bb, POOL_HALO + sq, d), F32)],
        compiler_params=pltpu.CompilerParams(dimension_semantics=_arb(1)),
        name="pool_sample",
    )(x, buf, g, w_pool_bf, s_pool)


def _router_kernel(x_ref, g_ref, wr_ref, br_ref, xn_ref, e_ref, gate_ref):
    xn = _rms(x_ref[...], g_ref[...])
    xn_ref[...] = xn
    logits = jnp.dot(xn, wr_ref[...], preferred_element_type=F32,
                     precision=lax.Precision.HIGHEST) + br_ref[...]
    tm, ne = logits.shape
    lane = lax.broadcasted_iota(I32, (tm, ne), 1)
    kcol = lax.broadcasted_iota(I32, (tm, TOP_K), 1)
    work = logits
    idx_out = jnp.zeros((tm, TOP_K), I32)
    val_out = jnp.zeros((tm, TOP_K), F32)
    for k in range(TOP_K):
        m = jnp.max(work, axis=-1, keepdims=True)
        idx = jnp.min(jnp.where(work == m, lane, ne), axis=-1, keepdims=True)
        idx_out = jnp.where(kcol == k, idx, idx_out)
        val_out = jnp.where(kcol == k, m, val_out)
        work = jnp.where(lane == idx, -jnp.inf, work)
    ex = jnp.exp(val_out - val_out[:, 0:1])
    e_ref[...] = idx_out
    gate_ref[...] = ex / jnp.sum(ex, axis=-1, keepdims=True)


def _router(x, g, w_router, b_router):
    t, d = x.shape
    tm = ROW_TILE
    ne = w_router.shape[1]
    return pl.pallas_call(
        _router_kernel,
        grid=(t // tm,),
        in_specs=[
            pl.BlockSpec((tm, d), lambda i: (i, 0)),
            pl.BlockSpec((1, d), lambda i: (0, 0)),
            pl.BlockSpec((d, ne), lambda i: (0, 0)),
            pl.BlockSpec((1, ne), lambda i: (0, 0)),
        ],
        out_specs=[
            pl.BlockSpec((tm, d), lambda i: (i, 0)),
            pl.BlockSpec((tm, TOP_K), lambda i: (i, 0)),
            pl.BlockSpec((tm, TOP_K), lambda i: (i, 0)),
        ],
        out_shape=[
            jax.ShapeDtypeStruct((t, d), F32),
            jax.ShapeDtypeStruct((t, TOP_K), I32),
            jax.ShapeDtypeStruct((t, TOP_K), F32),
        ],
        compiler_params=pltpu.CompilerParams(dimension_semantics=_arb(1)),
        name="router",
    )(x, g, w_router, b_router)


def _moe_tables(e4, t, nb_max, nf):
    tm, rb = MOE_TM, MOE_RB
    ne = N_EXPERTS
    sel = (e4[:, :, None] == jnp.arange(ne, dtype=I32)[None, None, :]).any(axis=1)
    seli = sel.astype(I32)
    counts = seli.sum(axis=0)
    nb = (counts + tm - 1) // tm
    bend = jnp.cumsum(nb)
    bstart = bend - nb
    rank = jnp.cumsum(seli, axis=0) - seli
    dest = bstart[None, :] * tm + rank
    pos4 = jnp.take_along_axis(dest, e4, axis=1).astype(I32)
    n_rows = nb_max * tm
    tok = jnp.broadcast_to(jnp.arange(t, dtype=I32)[:, None], (t, TOP_K))
    row_tok = jnp.zeros((n_rows,), I32).at[pos4.reshape(-1)].set(tok.reshape(-1), unique_indices=True)
    ns = nb_max * nf
    steps_e = nf * nb
    send = jnp.cumsum(steps_e)
    sstart = send - steps_e
    total = send[-1]
    s = jnp.arange(ns, dtype=I32)
    sc = jnp.minimum(s, total - 1)
    e = jnp.minimum((send[None, :] <= sc[:, None]).astype(I32).sum(axis=1), ne - 1)
    local = sc - sstart[e]
    per_group = nf * rb
    gi = local // per_group
    rem = local - gi * per_group
    gnb = jnp.minimum(rb, nb[e] - gi * rb)
    j = rem // gnb
    r = rem - j * gnb
    gfirst = bstart[e] + gi * rb
    blk = gfirst + r
    oblk = jnp.where(j == nf - 1, blk, gfirst)
    n_used = bend[-1]
    spare = s - total
    fill = jnp.logical_and(spare >= 0, spare < nb_max - n_used)
    oblk = jnp.where(s < total, oblk, jnp.minimum(n_used + jnp.maximum(spare, 0), nb_max - 1))
    nxt = sc + gnb
    has_next = jnp.logical_and(nxt < total, r == 0)
    nxt = jnp.minimum(nxt, ns - 1)
    flag = ((s < total).astype(I32) + 2 * (r == 0).astype(I32) + 4 * fill.astype(I32)
            + 8 * has_next.astype(I32))
    nbu = bend[-1:].astype(I32)
    return dict(pos4=pos4, row_tok=row_tok, nbu=nbu,
                st_e=e.astype(I32), st_j=j.astype(I32), st_blk=blk.astype(I32),
                st_oblk=oblk.astype(I32), st_r=r.astype(I32), st_flag=flag,
                st_ne=e[nxt].astype(I32), st_nj=j[nxt].astype(I32))


def _row_copy(src_hbm, dst, src_row, dst_row, sem):
    return pltpu.make_async_copy(src_hbm.at[pl.ds(src_row, 1)], dst.at[pl.ds(dst_row, 1)], sem)


def _gather_kernel(nbu_ref, tok_all, x_hbm, o_ref, tok_smem, gbuf, sem_idx, sem):
    i = pl.program_id(0)
    tm = o_ref.shape[0]
    nbu = nbu_ref[0]
    slot = lax.rem(i, 2)

    def idx_copy(blk, sl):
        return pltpu.make_async_copy(tok_all.at[blk], tok_smem.at[sl], sem_idx.at[sl])

    def gathers(sl):
        def body(r, c):
            _row_copy(x_hbm, gbuf.at[sl], tok_smem[sl, 0, r], r, sem.at[sl]).start()
            return c
        lax.fori_loop(0, tm, body, 0, unroll=8)

    @pl.when(jnp.logical_and(i == 0, nbu > 0))
    def _():
        idx_copy(0, 0).start()
        idx_copy(0, 0).wait()
        gathers(0)

        @pl.when(nbu > 1)
        def _():
            idx_copy(1, 1).start()

    for par in range(2):
        @pl.when(jnp.logical_and(i + 1 < nbu, slot == par))
        def _(par=par):
            idx_copy(i + 1, 1 - par).wait()
            gathers(1 - par)

    @pl.when(i + 2 < nbu)
    def _():
        idx_copy(i + 2, slot).start()

    @pl.when(i < nbu)
    def _():
        pltpu.make_async_copy(x_hbm.at[pl.ds(0, tm)], gbuf.at[slot], sem.at[slot]).wait()
        o_ref[...] = gbuf[slot].astype(BF16)

    @pl.when(i >= nbu)
    def _():
        o_ref[...] = jnp.zeros(o_ref.shape, o_ref.dtype)


def _dispatch_gather(xn, row_tok, nbu, nb_max):
    t, d = xn.shape
    tm = MOE_TM
    tok3 = row_tok.reshape(nb_max, 1, tm)
    return pl.pallas_call(
        _gather_kernel,
        grid_spec=pltpu.PrefetchScalarGridSpec(
            num_scalar_prefetch=1,
            grid=(nb_max,),
            in_specs=[
                pl.BlockSpec((nb_max, 1, tm), lambda i, nbu: (0, 0, 0)),
                pl.BlockSpec(memory_space=pl.ANY),
            ],
            out_specs=pl.BlockSpec((tm, d), lambda i, nbu: (i, 0)),
            scratch_shapes=[
                pltpu.SMEM((2, 1, tm), I32),
                pltpu.VMEM((2, tm, d), F32),
                pltpu.SemaphoreType.DMA((2,)), pltpu.SemaphoreType.DMA((2,)),
            ],
        ),
        out_shape=jax.ShapeDtypeStruct((nb_max * tm, d), BF16),
        compiler_params=pltpu.CompilerParams(dimension_semantics=_arb(1)),
        name="moe_dispatch",
    )(nbu, tok3, xn)


def _expert_kernel(nf, layer, st_e, st_j, st_blk, st_oblk, st_r, st_flag, st_ne, st_nj,
                   x_ref, bg_ref, bl_ref, bd_ref, wup_hbm, wdn_hbm, o_ref,
                   stg_g, stg_l, stg_d, wg_s, wl_s, wd_s, acc_s, sem):
    s = pl.program_id(0)
    flag = st_flag[s]
    j = st_j[s]
    r = st_r[s]
    tf = stg_g.shape[1]

    def tile_copies(e, jj):
        cg = pl.ds(pl.multiple_of(jj * tf, tf), tf)
        cl = pl.ds(pl.multiple_of((nf + jj) * tf, tf), tf)
        return (pltpu.make_async_copy(wup_hbm.at[layer, e, :, cg], stg_g, sem.at[0]),
                pltpu.make_async_copy(wup_hbm.at[layer, e, :, cl], stg_l, sem.at[1]),
                pltpu.make_async_copy(wdn_hbm.at[layer, e, cg, :], stg_d, sem.at[2]))

    @pl.when(s == 0)
    def _():
        for cp in tile_copies(st_e[0], st_j[0]):
            cp.start()

    @pl.when((flag & 4) != 0)
    def _():
        o_ref[...] = jnp.zeros(o_ref.shape, o_ref.dtype)

    @pl.when((flag & 1) != 0)
    def _():
        @pl.when((flag & 2) != 0)
        def _():
            cg, cl, cd = tile_copies(st_e[s], j)
            cg.wait()
            wg_s[...] = stg_g[...].astype(BF16)
            cl.wait()
            wl_s[...] = stg_l[...].astype(BF16)
            cd.wait()
            wd_s[...] = stg_d[...].astype(BF16)

            @pl.when((flag & 8) != 0)
            def _():
                for cp in tile_copies(st_ne[s], st_nj[s]):
                    cp.start()

        x = x_ref[...]
        hg = _dot(x, wg_s[...]) + bg_ref[...]
        hl = _dot(x, wl_s[...]) + bl_ref[...]
        hg = jnp.minimum(hg, SWIGLU_LIMIT)
        hl = jnp.clip(hl, -SWIGLU_LIMIT, SWIGLU_LIMIT)
        a = hg * jax.nn.sigmoid(SWIGLU_ALPHA * hg) * (hl + 1.0)
        c = _dot(a.astype(BF16), wd_s[...])

        if nf == 1:
            o_ref[...] = c + bd_ref[...]
        else:
            @pl.when(j == 0)
            def _():
                acc_s[r] = c + bd_ref[...]

            @pl.when(jnp.logical_and(j > 0, j < nf - 1))
            def _():
                acc_s[r] += c

            @pl.when(j == nf - 1)
            def _():
                o_ref[...] = acc_s[r] + c


def _expert_mlp(x_sorted, tabs, layer, w_up, b_up, w_down, b_down, nb_max):
    n_rows, d = x_sorted.shape
    ne, f2 = b_up.shape
    f = f2 // 2
    tm, tf, rb = MOE_TM, MOE_TF, MOE_RB
    nf = f // tf
    ns = nb_max * nf
    b_up3 = b_up.reshape(ne, 1, f2)
    b_down3 = b_down.reshape(ne, 1, d)
    x_bytes = jnp.dtype(x_sorted.dtype).itemsize
    vmem = (3 * d * tf * 4) + (3 * d * tf * 2) + rb * tm * d * 4 + 2 * tm * d * x_bytes + 2 * tm * d * 4 + (6 << 20)
    assert vmem <= V7X_VMEM_LIMIT, vmem
    idx = lambda f_: (lambda s, e, j, b, ob, r, fl, ne_, nj_: f_(s, e, j, b, ob))
    return pl.pallas_call(
        functools.partial(_expert_kernel, nf, layer),
        grid_spec=pltpu.PrefetchScalarGridSpec(
            num_scalar_prefetch=8,
            grid=(ns,),
            in_specs=[
                pl.BlockSpec((tm, d), idx(lambda s, e, j, b, ob: (b[s], 0))),
                pl.BlockSpec((None, 1, tf), idx(lambda s, e, j, b, ob: (e[s], 0, j[s]))),
                pl.BlockSpec((None, 1, tf), idx(lambda s, e, j, b, ob: (e[s], 0, nf + j[s]))),
                pl.BlockSpec((None, 1, d), idx(lambda s, e, j, b, ob: (e[s], 0, 0))),
                pl.BlockSpec(memory_space=pl.ANY),
                pl.BlockSpec(memory_space=pl.ANY),
            ],
            out_specs=pl.BlockSpec((tm, d), idx(lambda s, e, j, b, ob: (ob[s], 0))),
            scratch_shapes=[
                pltpu.VMEM((d, tf), F32), pltpu.VMEM((d, tf), F32), pltpu.VMEM((tf, d), F32),
                pltpu.VMEM((d, tf), BF16), pltpu.VMEM((d, tf), BF16), pltpu.VMEM((tf, d), BF16),
                pltpu.VMEM((rb, tm, d), F32),
                pltpu.SemaphoreType.DMA((3,)),
            ],
        ),
        out_shape=jax.ShapeDtypeStruct((n_rows, d), F32),
        compiler_params=pltpu.CompilerParams(dimension_semantics=_arb(1), vmem_limit_bytes=vmem),
        name="moe_experts",
    )(tabs['st_e'], tabs['st_j'], tabs['st_blk'], tabs['st_oblk'], tabs['st_r'], tabs['st_flag'],
      tabs['st_ne'], tabs['st_nj'],
      x_sorted, b_up3, b_up3, b_down3, w_up, w_down)


def _combine_kernel(final_norm, pos_all, g_ref, x_ref, gf_ref, y_hbm, o_ref, pos_smem, ybuf, sem_idx, sem):
    i = pl.program_id(0)
    nt = pl.num_programs(0)
    tt = x_ref.shape[0]
    slot = lax.rem(i, 2)

    def issue(tile, sl):
        cp = pltpu.make_async_copy(pos_all.at[tile], pos_smem.at[sl], sem_idx)
        cp.start()
        cp.wait()
        for k in range(TOP_K):
            def body(r, c, k=k):
                _row_copy(y_hbm, ybuf.at[sl, k], pos_smem[sl, 0, k * tt + r], r, sem.at[sl]).start()
                return c
            lax.fori_loop(0, tt, body, 0, unroll=8)

    @pl.when(i == 0)
    def _():
        issue(0, 0)

    for par in range(2):
        @pl.when(jnp.logical_and(i + 1 < nt, slot == par))
        def _(par=par):
            issue(i + 1, 1 - par)

    for k in range(TOP_K):
        pltpu.make_async_copy(y_hbm.at[pl.ds(0, tt)], ybuf.at[slot, k], sem.at[slot]).wait()
    g = g_ref[...]
    acc = x_ref[...]
    for k in range(TOP_K):
        acc = acc + g[:, k:k + 1] * ybuf[slot, k]
    if final_norm:
        acc = _rms(acc, gf_ref[...])
    o_ref[...] = acc


def _combine(x, y_sorted, pos4, gates, g_final, final_norm):
    t, d = x.shape
    tt = COMBINE_TT
    nt = t // tt
    pos3 = pos4.reshape(nt, tt, TOP_K).transpose(0, 2, 1).reshape(nt, 1, TOP_K * tt)
    return pl.pallas_call(
        functools.partial(_combine_kernel, final_norm),
        grid=(nt,),
        in_specs=[
            pl.BlockSpec((nt, 1, TOP_K * tt), lambda i: (0, 0, 0)),
            pl.BlockSpec((tt, TOP_K), lambda i: (i, 0)),
            pl.BlockSpec((tt, d), lambda i: (i, 0)),
            pl.BlockSpec((1, d), lambda i: (0, 0)),
            pl.BlockSpec(memory_space=pl.ANY),
        ],
        out_specs=pl.BlockSpec((tt, d), lambda i: (i, 0)),
        out_shape=jax.ShapeDtypeStruct((t, d), F32),
        scratch_shapes=[
            pltpu.SMEM((2, 1, TOP_K * tt), I32),
            pltpu.VMEM((2, TOP_K, tt, d), F32),
            pltpu.SemaphoreType.DMA(()), pltpu.SemaphoreType.DMA((2,)),
        ],
        compiler_params=pltpu.CompilerParams(dimension_semantics=_arb(1), vmem_limit_bytes=32 << 20),
        name="moe_combine",
    )(pos3, gates, x, g_final, y_sorted)


def _moe_layer(x, layer, g_ffn, w_router, b_router, w_up, b_up, w_down, b_down, g_final, final_norm):
    t, d = x.shape
    f = w_up.shape[3] // 2
    nf = f // MOE_TF
    nb_max = -(-(t * TOP_K) // MOE_TM) + N_EXPERTS
    xn, e4, gates = _router(x, g_ffn, w_router, b_router)
    tabs = _moe_tables(e4, t, nb_max, nf)
    x_sorted = _dispatch_gather(xn, tabs['row_tok'], tabs['nbu'], nb_max)
    y_sorted = _expert_mlp(x_sorted, tabs, layer, w_up, b_up, w_down, b_down, nb_max)
    return _combine(x, y_sorted, tabs['pos4'], gates, g_final, final_norm)


def _proj_kernel(x_ref, cos_ref, sin_ref, cost_ref, sint_ref, gkv_ref, gmix_ref, gckv_ref, gq_ref,
                 wc_ref, wp_ref, wpr_ref, wdq_ref, wqn_ref, wqp_ref, wqpr_ref, wqnt_ref, wqpt_ref, wqprt_ref,
                 ckv_ref, kpe_ref, q_ref, qt_ref):
    x = x_ref[...]
    cos = cos_ref[...]
    sin = sin_ref[...]
    xkv = _rms(x, gkv_ref[...]).astype(BF16)
    ckv_ref[...] = _rms(_dot(xkv, wc_ref[...]), gckv_ref[...])
    kpe_ref[...] = _dot(xkv, wp_ref[...]) * cos + _dot(xkv, wpr_ref[...]) * sin
    xq = _rms(x, gmix_ref[...]).astype(BF16)
    cq = _rms(_dot(xq, wdq_ref[...]), gq_ref[...]).astype(BF16)
    qn = _dot(cq, wqn_ref[...]) * ATTN_SCALE
    qp = _dot(cq, wqp_ref[...])
    qpr = _dot(cq, wqpr_ref[...])
    for h in range(N_HEADS):
        q_ref[h, :, 0:D_NOPE] = qn[:, h * D_NOPE:(h + 1) * D_NOPE].astype(BF16)
        ph = qp[:, h * D_ROPE:(h + 1) * D_ROPE] * cos + qpr[:, h * D_ROPE:(h + 1) * D_ROPE] * sin
        q_ref[h, :, D_NOPE:D_QK] = (ph * ATTN_SCALE).astype(BF16)
    cost = cost_ref[...]
    sint = sint_ref[...]
    qnt = _dot_nt(wqnt_ref[...], cq) * (ATTN_SCALE * LOG2_E)
    qpt = _dot_nt(wqpt_ref[...], cq)
    qprt = _dot_nt(wqprt_ref[...], cq)
    for h in range(N_HEADS):
        qt_ref[h, 0:D_NOPE, :] = qnt[h * D_NOPE:(h + 1) * D_NOPE, :].astype(BF16)
        pht = qpt[h * D_ROPE:(h + 1) * D_ROPE, :] * cost + qprt[h * D_ROPE:(h + 1) * D_ROPE, :] * sint
        qt_ref[h, D_NOPE:D_QK, :] = (pht * (ATTN_SCALE * LOG2_E)).astype(BF16)


def _proj(x, cos, sin, g_kv, g_mix, g_ckv, g_q, wc, wp, wpr, wdq, wqn, wqp, wqpr):
    t, d = x.shape
    tm = ROW_TILE
    kvl = wc.shape[1]
    full = lambda a: pl.BlockSpec(a.shape, lambda i: (0,) * a.ndim)
    row = lambda n: pl.BlockSpec((tm, n), lambda i: (i, 0))
    col = lambda n: pl.BlockSpec((n, tm), lambda i: (0, i))
    consts = (g_kv, g_mix, g_ckv, g_q, wc, wp, wpr, wdq, wqn, wqp, wqpr, wqn.T, wqp.T, wqpr.T)
    return pl.pallas_call(
        _proj_kernel,
        grid=(t // tm,),
        in_specs=[row(d), row(D_ROPE), row(D_ROPE), col(D_ROPE), col(D_ROPE)] + [full(a) for a in consts],
        out_specs=[row(kvl), row(D_ROPE), pl.BlockSpec((N_HEADS, tm, D_QK), lambda i: (0, i, 0)),
                   pl.BlockSpec((N_HEADS, D_QK, tm), lambda i: (0, 0, i))],
        out_shape=[
            jax.ShapeDtypeStruct((t, kvl), F32),
            jax.ShapeDtypeStruct((t, D_ROPE), F32),
            jax.ShapeDtypeStruct((N_HEADS, t, D_QK), BF16),
            jax.ShapeDtypeStruct((N_HEADS, D_QK, t), BF16),
        ],
        compiler_params=pltpu.CompilerParams(dimension_semantics=_arb(1), vmem_limit_bytes=48 << 20),
        name="latent_q_proj",
    )(x, cos, sin, cos.T, sin.T, *consts)


def _kv_up_kernel(ckv_ref, kpe_ref, wuk_ref, wuv_ref, k_ref, v_ref):
    c = ckv_ref[...].astype(BF16)
    kn = _dot(c, wuk_ref[...])
    vv = _dot(c, wuv_ref[...])
    kp = kpe_ref[...].astype(BF16)
    for h in range(N_HEADS):
        k_ref[h, :, 0:D_NOPE] = kn[:, h * D_NOPE:(h + 1) * D_NOPE].astype(BF16)
        k_ref[h, :, D_NOPE:D_QK] = kp
        v_ref[h] = vv[:, h * D_V:(h + 1) * D_V].astype(BF16)


def _kv_up(ckv, kpe, s, wuk2, wuv2):
    tm = ROW_TILE
    kvl = ckv.shape[1]
    return pl.pallas_call(
        _kv_up_kernel,
        grid=(s // tm,),
        in_specs=[
            pl.BlockSpec((tm, kvl), lambda i: (i, 0)),
            pl.BlockSpec((tm, D_ROPE), lambda i: (i, 0)),
            pl.BlockSpec(wuk2.shape, lambda i: (0, 0)),
            pl.BlockSpec(wuv2.shape, lambda i: (0, 0)),
        ],
        out_specs=[
            pl.BlockSpec((N_HEADS, tm, D_QK), lambda i: (0, i, 0)),
            pl.BlockSpec((N_HEADS, tm, D_V), lambda i: (0, i, 0)),
        ],
        out_shape=[
            jax.ShapeDtypeStruct((N_HEADS, s, D_QK), BF16),
            jax.ShapeDtypeStruct((N_HEADS, s, D_V), BF16),
        ],
        compiler_params=pltpu.CompilerParams(dimension_semantics=_arb(1)),
        name="kv_up",
    )(ckv, kpe, wuk2, wuv2)


def _flash_kernel(ti_ref, tj_ref, qt_ref, k_ref, v_ref, o_ref, m_s, l_s, acc_s):
    n = pl.program_id(1)
    i = ti_ref[n]
    j = tj_ref[n]
    ng, _, tq = qt_ref.shape
    tk = k_ref.shape[1]

    @pl.when(j == 0)
    def _():
        m_s[...] = jnp.full(m_s.shape, -jnp.inf, F32)
        l_s[...] = jnp.zeros(l_s.shape, F32)
        acc_s[...] = jnp.zeros(acc_s.shape, F32)

    def chain(u, masked):
        st = _dot(k_ref[u], qt_ref[u])
        if masked:
            key = lax.broadcasted_iota(I32, (tk, tq), 0)
            qry = lax.broadcasted_iota(I32, (tk, tq), 1)
            st = jnp.where(key <= qry, st, -jnp.inf)
        m_prev = m_s[u]
        m_new = jnp.maximum(m_prev, jnp.max(st, axis=0, keepdims=True))
        corr = jnp.exp2(m_prev - m_new)
        pt = jnp.exp2(st - m_new)
        l_s[u] = l_s[u] * corr + jnp.sum(pt, axis=0, keepdims=True)
        acc_s[u] = acc_s[u] * corr + _dot_tn(v_ref[u], pt.astype(BF16))
        m_s[u] = m_new

    @pl.when(j < i)
    def _():
        for u in range(ng):
            chain(u, False)

    @pl.when(j == i)
    def _():
        for u in range(ng):
            chain(u, True)
            o_ref[:, u * D_V:(u + 1) * D_V] = (acc_s[u] / l_s[u]).T.astype(o_ref.dtype)


def _flash_prompt(q_t, k_cat, v, s):
    tq = ATT_TQ
    ng = ATT_HEAD_GROUP
    assert tq == ATT_TK and s % tq == 0 and N_HEADS % ng == 0
    nq = s // tq
    pairs = [(i, j) for i in range(nq) for j in range(i + 1)]
    ti = jnp.asarray([p[0] for p in pairs], I32)
    tj = jnp.asarray([p[1] for p in pairs], I32)
    return pl.pallas_call(
        _flash_kernel,
        grid_spec=pltpu.PrefetchScalarGridSpec(
            num_scalar_prefetch=2,
            grid=(N_HEADS // ng, len(pairs)),
            in_specs=[
                pl.BlockSpec((ng, D_QK, tq), lambda g, n, ti, tj: (g, 0, ti[n])),
                pl.BlockSpec((ng, tq, D_QK), lambda g, n, ti, tj: (g, tj[n], 0)),
                pl.BlockSpec((ng, tq, D_V), lambda g, n, ti, tj: (g, tj[n], 0)),
            ],
            out_specs=pl.BlockSpec((tq, ng * D_V), lambda g, n, ti, tj: (ti[n], g)),
            scratch_shapes=[
                pltpu.VMEM((ng, 1, tq), F32), pltpu.VMEM((ng, 1, tq), F32),
                pltpu.VMEM((ng, D_V, tq), F32),
            ],
        ),
        out_shape=jax.ShapeDtypeStruct((s, N_HEADS * D_V), BF16),
        compiler_params=pltpu.CompilerParams(dimension_semantics=_arb(2), vmem_limit_bytes=48 << 20),
        name="flash_prompt",
    )(ti, tj, q_t, k_cat, v)


def _q_absorb_kernel(q_ref, wukt_ref, qlat_ref, qpe_ref):
    q = q_ref[...]
    qlat_ref[...] = _dot(q[:, 0:D_NOPE], wukt_ref[...])
    qpe_ref[...] = q[:, D_NOPE:D_QK].astype(F32)


def _q_absorb(q_cat, row0, ts, wukt):
    kvl = wukt.shape[2]
    assert row0 % ts == 0
    rb = row0 // ts
    return pl.pallas_call(
        _q_absorb_kernel,
        grid=(N_HEADS,),
        in_specs=[
            pl.BlockSpec((None, ts, D_QK), lambda h: (h, rb, 0)),
            pl.BlockSpec((None, D_NOPE, kvl), lambda h: (h, 0, 0)),
        ],
        out_specs=[
            pl.BlockSpec((None, ts, kvl), lambda h: (h, 0, 0)),
            pl.BlockSpec((None, ts, D_ROPE), lambda h: (h, 0, 0)),
        ],
        out_shape=[
            jax.ShapeDtypeStruct((N_HEADS, ts, kvl), F32),
            jax.ShapeDtypeStruct((N_HEADS, ts, D_ROPE), F32),
        ],
        compiler_params=pltpu.CompilerParams(dimension_semantics=_arb(1)),
        name="q_absorb",
    )(q_cat, wukt)


def _dec_attn_kernel(n_chunks, pt_ref, qlat_ref, qpe_ref, cnew_ref, pnew_ref, ckv_hbm, kpe_hbm, o_ref,
                     cbuf, pbuf, sem, m_s, l_s, acc_s, qlt_s, qpt_s, cn_s, pn_s, st_s):
    b = pl.program_id(0)
    c = pl.program_id(1)
    nb = pl.num_programs(0)
    n = b * n_chunks + c
    slot = lax.rem(n, 2)
    npg = DEC_PAGES
    nh, sq, kvl = qlat_ref.shape
    rows = nh * sq

    def fetch(step, slot_):
        bb = step // n_chunks
        cc = step - bb * n_chunks

        def body(i, carry):
            page = pt_ref[bb, cc * npg + i]
            pltpu.make_async_copy(ckv_hbm.at[page], cbuf.at[slot_, i], sem.at[0, slot_]).start()
            pltpu.make_async_copy(kpe_hbm.at[page], pbuf.at[slot_, i], sem.at[1, slot_]).start()
            return carry
        lax.fori_loop(0, npg, body, 0, unroll=8)

    @pl.when(n == 0)
    def _():
        fetch(n, slot)

    @pl.when(n + 1 < nb * n_chunks)
    def _():
        fetch(n + 1, 1 - slot)

    def to_col(v):
        eye = (lax.broadcasted_iota(I32, (rows, rows), 0) == lax.broadcasted_iota(I32, (rows, rows), 1))
        return jnp.sum(jnp.where(eye, jnp.broadcast_to(v, (rows, rows)), 0.0), axis=1, keepdims=True)

    @pl.when(c == 0)
    def _():
        qlt_s[...] = qlat_ref[...].reshape(rows, kvl).T.astype(BF16)
        qpt_s[...] = qpe_ref[...].reshape(rows, D_ROPE).T.astype(BF16)
        cn_s[...] = jnp.zeros(cn_s.shape, BF16)
        pn_s[...] = jnp.zeros(pn_s.shape, BF16)
        cn_s[0:sq, :] = cnew_ref[...].astype(BF16)
        pn_s[0:sq, :] = pnew_ref[...].astype(BF16)
        cn = cn_s[...]
        st = _dot(cn, qlt_s[...]) + _dot(pn_s[...], qpt_s[...])
        tkey = lax.broadcasted_iota(I32, st.shape, 0)
        srow = lax.rem(lax.broadcasted_iota(I32, st.shape, 1), sq)
        st = jnp.where(tkey <= srow, st, -jnp.inf)
        m = jnp.max(st, axis=0, keepdims=True)
        pt = jnp.exp(st - m)
        m_s[...] = m
        l_s[...] = jnp.sum(pt, axis=0, keepdims=True)
        acc_s[...] = _dot_tn(pt.astype(BF16), cn)

    pltpu.make_async_copy(ckv_hbm.at[pl.ds(0, npg)], cbuf.at[slot], sem.at[0, slot]).wait()
    pltpu.make_async_copy(kpe_hbm.at[pl.ds(0, npg)], pbuf.at[slot], sem.at[1, slot]).wait()

    sub = DEC_SUB_PAGES
    nsub = npg // sub
    qlt = qlt_s[...]
    qpt = qpt_s[...]

    def sub_block(u):
        ck = cbuf[slot, u * sub:(u + 1) * sub].reshape(sub * PAGE_SIZE, kvl).astype(BF16)
        kpt = jnp.concatenate([pbuf[slot, u * sub + i] for i in range(sub)], axis=1).astype(BF16)
        return ck, kpt

    m_prev = m_s[...]
    m_new = m_prev
    for u in range(nsub):
        ck, kpt = sub_block(u)
        st = _dot(ck, qlt) + _dot_tn(kpt, qpt)
        st_s[u] = st
        m_new = jnp.maximum(m_new, jnp.max(st, axis=0, keepdims=True))
    l_step = jnp.zeros_like(m_new)
    acc_step = jnp.zeros(acc_s.shape, F32)
    for u in range(nsub):
        ck, _ = sub_block(u)
        pt = jnp.exp(st_s[u] - m_new)
        l_step = l_step + jnp.sum(pt, axis=0, keepdims=True)
        acc_step = acc_step + _dot_tn(pt.astype(BF16), ck)
    corr = jnp.exp(m_prev - m_new)
    l_new = l_s[...] * corr + l_step
    acc_new = acc_s[...] * to_col(corr) + acc_step
    m_s[...] = m_new
    l_s[...] = l_new
    acc_s[...] = acc_new

    @pl.when(c == n_chunks - 1)
    def _():
        o_ref[...] = (acc_new / to_col(l_new)).reshape(nh, sq, kvl)


def _dec_attn(q_lat, q_pe, ckv_new, kpe_new, cache_ckv, cache_kpe_t, page_table):
    nh, ts, kvl = q_lat.shape
    bsz, n_pages = page_table.shape
    sq = ts // bsz
    assert n_pages % DEC_PAGES == 0 and sq == V7X_SUBLANES
    n_chunks = n_pages // DEC_PAGES
    rows = nh * sq
    assert DEC_PAGES % DEC_SUB_PAGES == 0 and cache_kpe_t.shape[1:] == (D_ROPE, PAGE_SIZE)
    return pl.pallas_call(
        functools.partial(_dec_attn_kernel, n_chunks),
        grid_spec=pltpu.PrefetchScalarGridSpec(
            num_scalar_prefetch=1,
            grid=(bsz, n_chunks),
            in_specs=[
                pl.BlockSpec((nh, sq, kvl), lambda b, c, pt: (0, b, 0)),
                pl.BlockSpec((nh, sq, D_ROPE), lambda b, c, pt: (0, b, 0)),
                pl.BlockSpec((sq, kvl), lambda b, c, pt: (b, 0)),
                pl.BlockSpec((sq, D_ROPE), lambda b, c, pt: (b, 0)),
                pl.BlockSpec(memory_space=pl.ANY),
                pl.BlockSpec(memory_space=pl.ANY),
            ],
            out_specs=pl.BlockSpec((nh, sq, kvl), lambda b, c, pt: (0, b, 0)),
            scratch_shapes=[
                pltpu.VMEM((2, DEC_PAGES, PAGE_SIZE, kvl), F32),
                pltpu.VMEM((2, DEC_PAGES, D_ROPE, PAGE_SIZE), F32),
                pltpu.SemaphoreType.DMA((2, 2)),
                pltpu.VMEM((1, rows), F32), pltpu.VMEM((1, rows), F32), pltpu.VMEM((rows, kvl), F32),
                pltpu.VMEM((kvl, rows), BF16), pltpu.VMEM((D_ROPE, rows), BF16),
                pltpu.VMEM((PAGE_SIZE, kvl), BF16), pltpu.VMEM((PAGE_SIZE, D_ROPE), BF16),
                pltpu.VMEM((DEC_PAGES // DEC_SUB_PAGES, DEC_SUB_PAGES * PAGE_SIZE, rows), F32),
            ],
        ),
        out_shape=jax.ShapeDtypeStruct((nh, ts, kvl), F32),
        compiler_params=pltpu.CompilerParams(dimension_semantics=_arb(2), vmem_limit_bytes=V7X_VMEM_LIMIT),
        name="dec_attn",
    )(page_table, q_lat, q_pe, ckv_new, kpe_new, cache_ckv, cache_kpe_t)


def _v_up_kernel(ol_ref, wuv_ref, o_ref):
    o_ref[...] = _dot(ol_ref[...].astype(BF16), wuv_ref[...]).astype(o_ref.dtype)


def _v_up(o_lat, wuv_h):
    nh, ts, kvl = o_lat.shape
    return pl.pallas_call(
        _v_up_kernel,
        grid=(nh,),
        in_specs=[
            pl.BlockSpec((None, ts, kvl), lambda h: (h, 0, 0)),
            pl.BlockSpec((None, kvl, D_V), lambda h: (h, 0, 0)),
        ],
        out_specs=pl.BlockSpec((ts, D_V), lambda h: (0, h)),
        out_shape=jax.ShapeDtypeStruct((ts, nh * D_V), BF16),
        compiler_params=pltpu.CompilerParams(dimension_semantics=_arb(1)),
        name="v_up",
    )(o_lat, wuv_h)


def _oproj_kernel(o_ref, x_ref, w_ref, y_ref):
    y_ref[...] = x_ref[...] + _dot(o_ref[...], w_ref[...])


def _oproj(o, x, w_o_bf):
    t, d = x.shape
    tm = ROW_TILE
    return pl.pallas_call(
        _oproj_kernel,
        grid=(t // tm,),
        in_specs=[
            pl.BlockSpec((tm, o.shape[1]), lambda i: (i, 0)),
            pl.BlockSpec((tm, d), lambda i: (i, 0)),
            pl.BlockSpec(w_o_bf.shape, lambda i: (0, 0)),
        ],
        out_specs=pl.BlockSpec((tm, d), lambda i: (i, 0)),
        out_shape=jax.ShapeDtypeStruct((t, d), F32),
        compiler_params=pltpu.CompilerParams(dimension_semantics=_arb(1), vmem_limit_bytes=40 << 20),
        name="attn_out_proj",
    )(o, x, w_o_bf)


def _rot_half_cols(w):
    half = D_ROPE // 2
    return jnp.concatenate([-w[..., half:], w[..., :half]], axis=-1)


def _rope_tables(pos):
    half = D_ROPE // 2
    inv_freq = jnp.power(ROPE_BASE, -jnp.arange(half, dtype=F32) / half)
    ang = pos.astype(F32)[:, None] * inv_freq[None, :]
    cos = jnp.cos(ang)
    sin = jnp.sin(ang)
    return jnp.concatenate([cos, cos], axis=-1), jnp.concatenate([sin, sin], axis=-1)


def kernel(x_prompt, x_sample, state_pool, cache_ckv, cache_kpe, page_table, g_mix, g_ffn, w_pool, s_pool,
           g_kv, w_dkv, g_ckv, w_uk, w_uv, w_dq, g_q, w_uq, w_o, w_router, b_router, w_up, b_up,
           w_down, b_down, g_final):
    bp, s, d = x_prompt.shape
    bd, sq, _ = x_sample.shape
    assert bp == 1 and g_mix.shape[0] == 2 and state_pool.shape[0] == 1
    ts = bd * sq
    t = s + ts
    past_len = page_table.shape[1] * cache_ckv.shape[1]
    kvl = w_uk.shape[0]
    row2 = lambda v: v.reshape(1, -1)

    w_pool_bf = w_pool[0].astype(BF16)
    x1p, pool_p = _pool_prompt(x_prompt[0], row2(g_mix[0]), w_pool_bf, row2(s_pool[0]))
    x1s, pool_s = _pool_sample(x_sample, state_pool[0], row2(g_mix[0]), w_pool_bf, row2(s_pool[0]), past_len)
    x1 = jnp.concatenate([x1p, x1s.reshape(ts, d)], axis=0)

    x2 = _moe_layer(x1, 0, row2(g_ffn[0]), w_router[0], row2(b_router[0]), w_up, b_up[0], w_down, b_down[0],
                    row2(g_final), False)

    pos = jnp.concatenate([jnp.arange(s, dtype=I32),
                           jnp.tile(past_len + jnp.arange(sq, dtype=I32), bd)])
    cos, sin = _rope_tables(pos)
    w_dkv_c = w_dkv[:, :kvl].astype(BF16)
    w_dkv_p = w_dkv[:, kvl:]
    w_uq1 = w_uq[0]
    ql = w_uq1.shape[0]
    w_qn = w_uq1[:, :, :D_NOPE].reshape(ql, N_HEADS * D_NOPE).astype(BF16)
    w_qp = w_uq1[:, :, D_NOPE:]
    ckv, kpe, q_cat, q_t = _proj(
        x2, cos, sin, row2(g_kv), row2(g_mix[1]), row2(g_ckv), row2(g_q[0]),
        w_dkv_c, w_dkv_p.astype(BF16), _rot_half_cols(w_dkv_p).astype(BF16), w_dq[0].astype(BF16),
        w_qn, w_qp.reshape(ql, N_HEADS * D_ROPE).astype(BF16),
        _rot_half_cols(w_qp).reshape(ql, N_HEADS * D_ROPE).astype(BF16))

    wuk2 = w_uk.reshape(kvl, N_HEADS * D_NOPE).astype(BF16)
    wuv2 = w_uv.reshape(kvl, N_HEADS * D_V).astype(BF16)
    k_cat, v = _kv_up(ckv, kpe, s, wuk2, wuv2)
    o_p = _flash_prompt(q_t, k_cat, v, s)

    wukt = jnp.transpose(w_uk, (1, 2, 0)).astype(BF16)
    wuv_h = jnp.transpose(w_uv, (1, 0, 2)).astype(BF16)
    q_lat, q_pe = _q_absorb(q_cat, s, ts, wukt)
    ckv_s = ckv[s:]
    kpe_s = kpe[s:]
    o_lat = _dec_attn(q_lat, q_pe, ckv_s, kpe_s, cache_ckv, jnp.swapaxes(cache_kpe, 1, 2), page_table)
    o_s = _v_up(o_lat, wuv_h)

    o = jnp.concatenate([o_p, o_s], axis=0)
    x3 = _oproj(o, x2, w_o[0].astype(BF16))
    y = _moe_layer(x3, 1, row2(g_ffn[1]), w_router[1], row2(b_router[1]), w_up, b_up[1], w_down, b_down[1],
                   row2(g_final), True)

    return (y[:s].reshape(1, s, d), y[s:].reshape(bd, sq, d),
            pool_p.reshape(1, 1, POOL_BUF, d), pool_s.reshape(1, bd, POOL_BUF, d),
            ckv[:s].reshape(1, s, kvl), kpe[:s].reshape(1, s, D_ROPE),
            ckv_s.reshape(bd, sq, kvl), kpe_s.reshape(bd, sq, D_ROPE))
```

```python
import functools

import jax
import jax.numpy as jnp
from jax import lax
from jax.experimental import pallas as pl
from jax.experimental.pallas import tpu as pltpu

POOL_WINDOWS = (2, 4, 8, 16)
POOL_BUF = max(POOL_WINDOWS) - 1
N_HEADS = 16
D_NOPE = 128
D_ROPE = 64
D_QK = D_NOPE + D_ROPE
D_V = 128
ROPE_BASE = 10000.0
ATTN_SCALE = D_QK ** -0.5
LOG2_E = 1.4426950408889634
N_EXPERTS = 32
TOP_K = 4
SWIGLU_ALPHA = 1.702
SWIGLU_LIMIT = 7.0
EPS = 1e-6
PAGE_SIZE = 128

V7X_SUBLANES = 8
V7X_LANES = 128
V7X_VMEM_BYTES = 64 * 1024 * 1024
V7X_VMEM_LIMIT = 56 * 1024 * 1024

ROW_TILE = 256
POOL_HALO = 2 * V7X_SUBLANES
MOE_TM = 256
MOE_TF = 512
MOE_RB = 8
COMBINE_TT = 256
ATT_TQ = 512
ATT_TK = 512
ATT_HEAD_GROUP = 16
DEC_PAGES = 64
DEC_SUB_PAGES = 4

BF16 = jnp.bfloat16
F32 = jnp.float32
I32 = jnp.int32


def _arb(n):
    return ("arbitrary",) * n


def _rms(x, g):
    return x * lax.rsqrt(jnp.mean(x * x, axis=-1, keepdims=True) + EPS) * g


def _dot(a, b):
    return jnp.dot(a, b, preferred_element_type=F32)


def _dot_nt(a, b):
    return lax.dot_general(a, b, (((1,), (1,)), ((), ())), preferred_element_type=F32)


def _dot_tn(a, b):
    return lax.dot_general(a, b, (((0,), (0,)), ((), ())), preferred_element_type=F32)


def _pool_windows(cat_ref, xn, lead, rows, start_pos, w_ref, s_ref):
    d = xn.shape[-1]
    grp = d // len(POOL_WINDOWS)
    t_axis = xn.ndim - 2
    pos = start_pos + lax.broadcasted_iota(I32, xn.shape[:-1] + (1,), t_axis)
    outs = []
    for g, w in enumerate(POOL_WINDOWS):
        ch = slice(g * grp, (g + 1) * grp)
        acc = None
        for k in range(w):
            piece = cat_ref[lead + (pl.ds(POOL_HALO - k, rows), ch)]
            acc = piece if acc is None else acc + piece
        cnt = jnp.minimum(pos + 1, w).astype(F32)
        dg = acc / cnt - xn[..., ch]
        dg2 = dg.reshape(-1, grp).astype(BF16)
        outs.append(_dot(dg2, w_ref[g]))
    y = jnp.concatenate(outs, axis=-1)
    return y * s_ref[...]


def _pool_prompt_kernel(x_ref, g_ref, w_ref, s_ref, o_ref, np_ref, cat_ref):
    i = pl.program_id(0)
    tm = x_ref.shape[0]

    @pl.when(i == 0)
    def _():
        cat_ref[pl.ds(0, POOL_HALO), :] = jnp.zeros((POOL_HALO, cat_ref.shape[1]), F32)

    x = x_ref[...]
    xn = _rms(x, g_ref[...])
    cat_ref[pl.ds(POOL_HALO, tm), :] = xn
    y = _pool_windows(cat_ref, xn, (), tm, i * tm, w_ref, s_ref)
    o_ref[...] = x + y
    np_ref[...] = cat_ref[pl.ds(POOL_HALO + tm - POOL_BUF, POOL_BUF), :]
    cat_ref[pl.ds(0, POOL_HALO), :] = cat_ref[pl.ds(tm, POOL_HALO), :]


def _pool_prompt(x, g, w_pool_bf, s_pool):
    s, d = x.shape
    tm = ROW_TILE
    assert s % tm == 0 and tm >= POOL_HALO
    return pl.pallas_call(
        _pool_prompt_kernel,
        grid=(s // tm,),
        in_specs=[
            pl.BlockSpec((tm, d), lambda i: (i, 0)),
            pl.BlockSpec((1, d), lambda i: (0, 0)),
            pl.BlockSpec(w_pool_bf.shape, lambda i: (0, 0, 0)),
            pl.BlockSpec((1, d), lambda i: (0, 0)),
        ],
        out_specs=[
            pl.BlockSpec((tm, d), lambda i: (i, 0)),
            pl.BlockSpec((POOL_BUF, d), lambda i: (0, 0)),
        ],
        out_shape=[jax.ShapeDtypeStruct((s, d), F32), jax.ShapeDtypeStruct((POOL_BUF, d), F32)],
        scratch_shapes=[pltpu.VMEM((POOL_HALO + tm, d), F32)],
        compiler_params=pltpu.CompilerParams(dimension_semantics=_arb(1)),
        name="pool_prompt",
    )(x, g, w_pool_bf, s_pool)


def _pool_sample_kernel(start_pos, x_ref, buf_ref, g_ref, w_ref, s_ref, o_ref, np_ref, cat_ref):
    bb, sq, d = x_ref.shape
    x = x_ref[...]
    xn = _rms(x, g_ref[...])
    cat_ref[:, pl.ds(POOL_HALO - POOL_BUF, POOL_BUF), :] = buf_ref[...]
    cat_ref[:, pl.ds(POOL_HALO, sq), :] = xn
    y = _pool_windows(cat_ref, xn, (slice(None),), sq, start_pos, w_ref, s_ref)
    o_ref[...] = x + y.reshape(bb, sq, d)
    np_ref[...] = cat_ref[:, pl.ds(POOL_HALO + sq - POOL_BUF, POOL_BUF), :]


def _pool_sample(x, buf, g, w_pool_bf, s_pool, start_pos):
    b, sq, d = x.shape
    bb = 8
    assert b % bb == 0 and sq == V7X_SUBLANES
    return pl.pallas_call(
        functools.partial(_pool_sample_kernel, start_pos),
        grid=(b // bb,),
        in_specs=[
            pl.BlockSpec((bb, sq, d), lambda i: (i, 0, 0)),
            pl.BlockSpec((bb, POOL_BUF, d), lambda i: (i, 0, 0)),
            pl.BlockSpec((1, d), lambda i: (0, 0)),
            pl.BlockSpec(w_pool_bf.shape, lambda i: (0, 0, 0)),
            pl.BlockSpec((1, d), lambda i: (0, 0)),
        ],
        out_specs=[
            pl.BlockSpec((bb, sq, d), lambda i: (i, 0, 0)),
            pl.BlockSpec((bb, POOL_BUF, d), lambda i: (i, 0, 0)),
        ],
        out_shape=[jax.ShapeDtypeStruct((b, sq, d), F32), jax.ShapeDtypeStruct((b, POOL_BUF, d), F32)],
        scratch_shapes=[pltpu.VMEM((bb, POOL_HALO + sq, d), F32)],
        compiler_params=pltpu.CompilerParams(dimension_semantics=_arb(1)),
        name="pool_sample",
    )(x, buf, g, w_pool_bf, s_pool)


def _router_kernel(x_ref, g_ref, wr_ref, br_ref, xn_ref, e_ref, gate_ref):
    xn = _rms(x_ref[...], g_ref[...])
    xn_ref[...] = xn
    wr = wr_ref[...]
    x_hi = xn.astype(BF16)
    x_lo = (xn - x_hi.astype(F32)).astype(BF16)
    w_hi = wr.astype(BF16)
    w_lo = (wr - w_hi.astype(F32)).astype(BF16)
    logits = _dot(x_hi, w_hi) + (_dot(x_lo, w_hi) + _dot(x_hi, w_lo)) + br_ref[...]
    tm, ne = logits.shape
    lane = lax.broadcasted_iota(I32, (tm, ne), 1)
    kcol = lax.broadcasted_iota(I32, (tm, TOP_K), 1)
    work = logits
    idx_out = jnp.zeros((tm, TOP_K), I32)
    val_out = jnp.zeros((tm, TOP_K), F32)
    for k in range(TOP_K):
        m = jnp.max(work, axis=-1, keepdims=True)
        idx = jnp.min(jnp.where(work == m, lane, ne), axis=-1, keepdims=True)
        idx_out = jnp.where(kcol == k, idx, idx_out)
        val_out = jnp.where(kcol == k, m, val_out)
        work = jnp.where(lane == idx, -jnp.inf, work)
    ex = jnp.exp(val_out - val_out[:, 0:1])
    e_ref[...] = idx_out
    gate_ref[...] = ex / jnp.sum(ex, axis=-1, keepdims=True)


def _router(x, g, w_router, b_router):
    t, d = x.shape
    tm = ROW_TILE
    ne = w_router.shape[1]
    return pl.pallas_call(
        _router_kernel,
        grid=(t // tm,),
        in_specs=[
            pl.BlockSpec((tm, d), lambda i: (i, 0)),
            pl.BlockSpec((1, d), lambda i: (0, 0)),
            pl.BlockSpec((d, ne), lambda i: (0, 0)),
            pl.BlockSpec((1, ne), lambda i: (0, 0)),
        ],
        out_specs=[
            pl.BlockSpec((tm, d), lambda i: (i, 0)),
            pl.BlockSpec((tm, TOP_K), lambda i: (i, 0)),
            pl.BlockSpec((tm, TOP_K), lambda i: (i, 0)),
        ],
        out_shape=[
            jax.ShapeDtypeStruct((t, d), F32),
            jax.ShapeDtypeStruct((t, TOP_K), I32),
            jax.ShapeDtypeStruct((t, TOP_K), F32),
        ],
        compiler_params=pltpu.CompilerParams(dimension_semantics=_arb(1)),
        name="router",
    )(x, g, w_router, b_router)


def _moe_tables(e4, t, nb_max, nf):
    tm, rb = MOE_TM, MOE_RB
    ne = N_EXPERTS
    sel = (e4[:, :, None] == jnp.arange(ne, dtype=I32)[None, None, :]).any(axis=1)
    seli = sel.astype(I32)
    counts = seli.sum(axis=0)
    nb = (counts + tm - 1) // tm
    bend = jnp.cumsum(nb)
    bstart = bend - nb
    rank = jnp.cumsum(seli, axis=0) - seli
    dest = bstart[None, :] * tm + rank
    pos4 = jnp.take_along_axis(dest, e4, axis=1).astype(I32)
    n_rows = nb_max * tm
    tok = jnp.broadcast_to(jnp.arange(t, dtype=I32)[:, None], (t, TOP_K))
    row_tok = jnp.zeros((n_rows,), I32).at[pos4.reshape(-1)].set(tok.reshape(-1), unique_indices=True)
    ns = nb_max * nf
    steps_e = nf * nb
    send = jnp.cumsum(steps_e)
    sstart = send - steps_e
    total = send[-1]
    s = jnp.arange(ns, dtype=I32)
    sc = jnp.minimum(s, total - 1)
    e = jnp.minimum((send[None, :] <= sc[:, None]).astype(I32).sum(axis=1), ne - 1)
    local = sc - sstart[e]
    per_group = nf * rb
    gi = local // per_group
    rem = local - gi * per_group
    gnb = jnp.minimum(rb, nb[e] - gi * rb)
    j = rem // gnb
    r = rem - j * gnb
    gfirst = bstart[e] + gi * rb
    blk = gfirst + r
    oblk = jnp.where(j == nf - 1, blk, gfirst)
    n_used = bend[-1]
    spare = s - total
    fill = jnp.logical_and(spare >= 0, spare < nb_max - n_used)
    oblk = jnp.where(s < total, oblk, jnp.minimum(n_used + jnp.maximum(spare, 0), nb_max - 1))
    nxt = sc + gnb
    has_next = jnp.logical_and(nxt < total, r == 0)
    nxt = jnp.minimum(nxt, ns - 1)
    flag = ((s < total).astype(I32) + 2 * (r == 0).astype(I32) + 4 * fill.astype(I32)
            + 8 * has_next.astype(I32))
    nbu = bend[-1:].astype(I32)
    return dict(pos4=pos4, row_tok=row_tok, nbu=nbu,
                st_e=e.astype(I32), st_j=j.astype(I32), st_blk=blk.astype(I32),
                st_oblk=oblk.astype(I32), st_r=r.astype(I32), st_flag=flag,
                st_ne=e[nxt].astype(I32), st_nj=j[nxt].astype(I32))


def _row_copy(src_hbm, dst, src_row, dst_row, sem):
    return pltpu.make_async_copy(src_hbm.at[pl.ds(src_row, 1)], dst.at[pl.ds(dst_row, 1)], sem)


def _gather_kernel(nbu_ref, tok_all, x_hbm, o_ref, tok_smem, gbuf, sem_idx, sem):
    i = pl.program_id(0)
    tm = o_ref.shape[0]
    nbu = nbu_ref[0]
    slot = lax.rem(i, 2)

    def idx_copy(blk, sl):
        return pltpu.make_async_copy(tok_all.at[blk], tok_smem.at[sl], sem_idx.at[sl])

    def gathers(sl):
        def body(r, c):
            _row_copy(x_hbm, gbuf.at[sl], tok_smem[sl, 0, r], r, sem.at[sl]).start()
            return c
        lax.fori_loop(0, tm, body, 0, unroll=8)

    @pl.when(jnp.logical_and(i == 0, nbu > 0))
    def _():
        idx_copy(0, 0).start()
        idx_copy(0, 0).wait()
        gathers(0)

        @pl.when(nbu > 1)
        def _():
            idx_copy(1, 1).start()

    for par in range(2):
        @pl.when(jnp.logical_and(i + 1 < nbu, slot == par))
        def _(par=par):
            idx_copy(i + 1, 1 - par).wait()
            gathers(1 - par)

    @pl.when(i + 2 < nbu)
    def _():
        idx_copy(i + 2, slot).start()

    @pl.when(i < nbu)
    def _():
        pltpu.make_async_copy(x_hbm.at[pl.ds(0, tm)], gbuf.at[slot], sem.at[slot]).wait()
        o_ref[...] = gbuf[slot].astype(BF16)

    @pl.when(i >= nbu)
    def _():
        o_ref[...] = jnp.zeros(o_ref.shape, o_ref.dtype)


def _dispatch_gather(xn, row_tok, nbu, nb_max):
    t, d = xn.shape
    tm = MOE_TM
    tok3 = row_tok.reshape(nb_max, 1, tm)
    return pl.pallas_call(
        _gather_kernel,
        grid_spec=pltpu.PrefetchScalarGridSpec(
            num_scalar_prefetch=1,
            grid=(nb_max,),
            in_specs=[
                pl.BlockSpec((nb_max, 1, tm), lambda i, nbu: (0, 0, 0)),
                pl.BlockSpec(memory_space=pl.ANY),
            ],
            out_specs=pl.BlockSpec((tm, d), lambda i, nbu: (i, 0)),
            scratch_shapes=[
                pltpu.SMEM((2, 1, tm), I32),
                pltpu.VMEM((2, tm, d), F32),
                pltpu.SemaphoreType.DMA((2,)), pltpu.SemaphoreType.DMA((2,)),
            ],
        ),
        out_shape=jax.ShapeDtypeStruct((nb_max * tm, d), BF16),
        compiler_params=pltpu.CompilerParams(dimension_semantics=_arb(1)),
        name="moe_dispatch",
    )(nbu, tok3, xn)


def _expert_kernel(nf, layer, st_e, st_j, st_blk, st_oblk, st_r, st_flag, st_ne, st_nj,
                   x_ref, bg_ref, bl_ref, bd_ref, wup_hbm, wdn_hbm, o_ref,
                   stg_g, stg_l, stg_d, wg_s, wl_s, wd_s, acc_s, sem):
    s = pl.program_id(0)
    flag = st_flag[s]
    j = st_j[s]
    r = st_r[s]
    tf = stg_g.shape[1]

    def tile_copies(e, jj):
        cg = pl.ds(pl.multiple_of(jj * tf, tf), tf)
        cl = pl.ds(pl.multiple_of((nf + jj) * tf, tf), tf)
        return (pltpu.make_async_copy(wup_hbm.at[layer, e, :, cg], stg_g, sem.at[0]),
                pltpu.make_async_copy(wup_hbm.at[layer, e, :, cl], stg_l, sem.at[1]),
                pltpu.make_async_copy(wdn_hbm.at[layer, e, cg, :], stg_d, sem.at[2]))

    @pl.when(s == 0)
    def _():
        for cp in tile_copies(st_e[0], st_j[0]):
            cp.start()

    @pl.when((flag & 4) != 0)
    def _():
        o_ref[...] = jnp.zeros(o_ref.shape, o_ref.dtype)

    @pl.when((flag & 1) != 0)
    def _():
        @pl.when((flag & 2) != 0)
        def _():
            cg, cl, cd = tile_copies(st_e[s], j)
            cg.wait()
            wg_s[...] = stg_g[...].astype(BF16)
            cl.wait()
            wl_s[...] = stg_l[...].astype(BF16)
            cd.wait()
            wd_s[...] = stg_d[...].astype(BF16)

            @pl.when((flag & 8) != 0)
            def _():
                for cp in tile_copies(st_ne[s], st_nj[s]):
                    cp.start()

        x = x_ref[...]
        hg = _dot(x, wg_s[...]) + bg_ref[...]
        hl = _dot(x, wl_s[...]) + bl_ref[...]
        hg = jnp.minimum(hg, SWIGLU_LIMIT)
        hl = jnp.clip(hl, -SWIGLU_LIMIT, SWIGLU_LIMIT)
        a = hg * jax.nn.sigmoid(SWIGLU_ALPHA * hg) * (hl + 1.0)
        c = _dot(a.astype(BF16), wd_s[...])

        if nf == 1:
            o_ref[...] = c + bd_ref[...]
        else:
            @pl.when(j == 0)
            def _():
                acc_s[r] = c + bd_ref[...]

            @pl.when(jnp.logical_and(j > 0, j < nf - 1))
            def _():
                acc_s[r] += c

            @pl.when(j == nf - 1)
            def _():
                o_ref[...] = acc_s[r] + c


def _expert_mlp(x_sorted, tabs, layer, w_up, b_up, w_down, b_down, nb_max):
    n_rows, d = x_sorted.shape
    ne, f2 = b_up.shape
    f = f2 // 2
    tm, tf, rb = MOE_TM, MOE_TF, MOE_RB
    nf = f // tf
    ns = nb_max * nf
    b_up3 = b_up.reshape(ne, 1, f2)
    b_down3 = b_down.reshape(ne, 1, d)
    x_bytes = jnp.dtype(x_sorted.dtype).itemsize
    vmem = (3 * d * tf * 4) + (3 * d * tf * 2) + rb * tm * d * 4 + 2 * tm * d * x_bytes + 2 * tm * d * 4 + (6 << 20)
    assert vmem <= V7X_VMEM_LIMIT, vmem
    idx = lambda f_: (lambda s, e, j, b, ob, r, fl, ne_, nj_: f_(s, e, j, b, ob))
    return pl.pallas_call(
        functools.partial(_expert_kernel, nf, layer),
        grid_spec=pltpu.PrefetchScalarGridSpec(
            num_scalar_prefetch=8,
            grid=(ns,),
            in_specs=[
                pl.BlockSpec((tm, d), idx(lambda s, e, j, b, ob: (b[s], 0))),
                pl.BlockSpec((None, 1, tf), idx(lambda s, e, j, b, ob: (e[s], 0, j[s]))),
                pl.BlockSpec((None, 1, tf), idx(lambda s, e, j, b, ob: (e[s], 0, nf + j[s]))),
                pl.BlockSpec((None, 1, d), idx(lambda s, e, j, b, ob: (e[s], 0, 0))),
                pl.BlockSpec(memory_space=pl.ANY),
                pl.BlockSpec(memory_space=pl.ANY),
            ],
            out_specs=pl.BlockSpec((tm, d), idx(lambda s, e, j, b, ob: (ob[s], 0))),
            scratch_shapes=[
                pltpu.VMEM((d, tf), F32), pltpu.VMEM((d, tf), F32), pltpu.VMEM((tf, d), F32),
                pltpu.VMEM((d, tf), BF16), pltpu.VMEM((d, tf), BF16), pltpu.VMEM((tf, d), BF16),
                pltpu.VMEM((rb, tm, d), F32),
                pltpu.SemaphoreType.DMA((3,)),
            ],
        ),
        out_shape=jax.ShapeDtypeStruct((n_rows, d), F32),
        compiler_params=pltpu.CompilerParams(dimension_semantics=_arb(1), vmem_limit_bytes=vmem),
        name="moe_experts",
    )(tabs['st_e'], tabs['st_j'], tabs['st_blk'], tabs['st_oblk'], tabs['st_r'], tabs['st_flag'],
      tabs['st_ne'], tabs['st_nj'],
      x_sorted, b_up3, b_up3, b_down3, w_up, w_down)


def _combine_kernel(final_norm, pos_all, g_ref, x_ref, gf_ref, y_hbm, o_ref, pos_smem, ybuf, sem_idx, sem):
    i = pl.program_id(0)
    nt = pl.num_programs(0)
    tt = x_ref.shape[0]
    slot = lax.rem(i, 2)

    def issue(tile, sl):
        cp = pltpu.make_async_copy(pos_all.at[tile], pos_smem.at[sl], sem_idx)
        cp.start()
        cp.wait()
        for k in range(TOP_K):
            def body(r, c, k=k):
                _row_copy(y_hbm, ybuf.at[sl, k], pos_smem[sl, 0, k * tt + r], r, sem.at[sl]).start()
                return c
            lax.fori_loop(0, tt, body, 0, unroll=8)

    @pl.when(i == 0)
    def _():
        issue(0, 0)

    for par in range(2):
        @pl.when(jnp.logical_and(i + 1 < nt, slot == par))
        def _(par=par):
            issue(i + 1, 1 - par)

    for k in range(TOP_K):
        pltpu.make_async_copy(y_hbm.at[pl.ds(0, tt)], ybuf.at[slot, k], sem.at[slot]).wait()
    g = g_ref[...]
    acc = x_ref[...]
    for k in range(TOP_K):
        acc = acc + g[:, k:k + 1] * ybuf[slot, k]
    if final_norm:
        acc = _rms(acc, gf_ref[...])
    o_ref[...] = acc


def _combine(x, y_sorted, pos4, gates, g_final, final_norm):
    t, d = x.shape
    tt = COMBINE_TT
    nt = t // tt
    pos3 = pos4.reshape(nt, tt, TOP_K).transpose(0, 2, 1).reshape(nt, 1, TOP_K * tt)
    return pl.pallas_call(
        functools.partial(_combine_kernel, final_norm),
        grid=(nt,),
        in_specs=[
            pl.BlockSpec((nt, 1, TOP_K * tt), lambda i: (0, 0, 0)),
            pl.BlockSpec((tt, TOP_K), lambda i: (i, 0)),
            pl.BlockSpec((tt, d), lambda i: (i, 0)),
            pl.BlockSpec((1, d), lambda i: (0, 0)),
            pl.BlockSpec(memory_space=pl.ANY),
        ],
        out_specs=pl.BlockSpec((tt, d), lambda i: (i, 0)),
        out_shape=jax.ShapeDtypeStruct((t, d), F32),
        scratch_shapes=[
            pltpu.SMEM((2, 1, TOP_K * tt), I32),
            pltpu.VMEM((2, TOP_K, tt, d), F32),
            pltpu.SemaphoreType.DMA(()), pltpu.SemaphoreType.DMA((2,)),
        ],
        compiler_params=pltpu.CompilerParams(dimension_semantics=_arb(1), vmem_limit_bytes=40 << 20),
        name="moe_combine",
    )(pos3, gates, x, g_final, y_sorted)


def _moe_layer(x, layer, g_ffn, w_router, b_router, w_up, b_up, w_down, b_down, g_final, final_norm):
    t, d = x.shape
    f = w_up.shape[3] // 2
    nf = f // MOE_TF
    nb_max = -(-(t * TOP_K) // MOE_TM) + N_EXPERTS
    xn, e4, gates = _router(x, g_ffn, w_router, b_router)
    tabs = _moe_tables(e4, t, nb_max, nf)
    x_sorted = _dispatch_gather(xn, tabs['row_tok'], tabs['nbu'], nb_max)
    y_sorted = _expert_mlp(x_sorted, tabs, layer, w_up, b_up, w_down, b_down, nb_max)
    return _combine(x, y_sorted, tabs['pos4'], gates, g_final, final_norm)


def _proj_kernel(x_ref, cos_ref, sin_ref, cost_ref, sint_ref, gkv_ref, gmix_ref, gckv_ref, gq_ref,
                 wc_ref, wp_ref, wpr_ref, wdq_ref, wqn_ref, wqp_ref, wqpr_ref, wqnt_ref, wqpt_ref, wqprt_ref,
                 ckv_ref, kpe_ref, q_ref, qt_ref):
    x = x_ref[...]
    cos = cos_ref[...]
    sin = sin_ref[...]
    xkv = _rms(x, gkv_ref[...]).astype(BF16)
    ckv_ref[...] = _rms(_dot(xkv, wc_ref[...]), gckv_ref[...])
    kpe_ref[...] = _dot(xkv, wp_ref[...]) * cos + _dot(xkv, wpr_ref[...]) * sin
    xq = _rms(x, gmix_ref[...]).astype(BF16)
    cq = _rms(_dot(xq, wdq_ref[...]), gq_ref[...]).astype(BF16)
    qn = _dot(cq, wqn_ref[...]) * ATTN_SCALE
    qp = _dot(cq, wqp_ref[...])
    qpr = _dot(cq, wqpr_ref[...])
    for h in range(N_HEADS):
        q_ref[h, :, 0:D_NOPE] = qn[:, h * D_NOPE:(h + 1) * D_NOPE].astype(BF16)
        ph = qp[:, h * D_ROPE:(h + 1) * D_ROPE] * cos + qpr[:, h * D_ROPE:(h + 1) * D_ROPE] * sin
        q_ref[h, :, D_NOPE:D_QK] = (ph * ATTN_SCALE).astype(BF16)
    cost = cost_ref[...]
    sint = sint_ref[...]
    qnt = _dot_nt(wqnt_ref[...], cq) * (ATTN_SCALE * LOG2_E)
    qpt = _dot_nt(wqpt_ref[...], cq)
    qprt = _dot_nt(wqprt_ref[...], cq)
    for h in range(N_HEADS):
        qt_ref[h, 0:D_NOPE, :] = qnt[h * D_NOPE:(h + 1) * D_NOPE, :].astype(BF16)
        pht = qpt[h * D_ROPE:(h + 1) * D_ROPE, :] * cost + qprt[h * D_ROPE:(h + 1) * D_ROPE, :] * sint
        qt_ref[h, D_NOPE:D_QK, :] = (pht * (ATTN_SCALE * LOG2_E)).astype(BF16)


def _proj(x, cos, sin, g_kv, g_mix, g_ckv, g_q, wc, wp, wpr, wdq, wqn, wqp, wqpr):
    t, d = x.shape
    tm = ROW_TILE
    kvl = wc.shape[1]
    full = lambda a: pl.BlockSpec(a.shape, lambda i: (0,) * a.ndim)
    row = lambda n: pl.BlockSpec((tm, n), lambda i: (i, 0))
    col = lambda n: pl.BlockSpec((n, tm), lambda i: (0, i))
    consts = (g_kv, g_mix, g_ckv, g_q, wc, wp, wpr, wdq, wqn, wqp, wqpr, wqn.T, wqp.T, wqpr.T)
    return pl.pallas_call(
        _proj_kernel,
        grid=(t // tm,),
        in_specs=[row(d), row(D_ROPE), row(D_ROPE), col(D_ROPE), col(D_ROPE)] + [full(a) for a in consts],
        out_specs=[row(kvl), row(D_ROPE), pl.BlockSpec((N_HEADS, tm, D_QK), lambda i: (0, i, 0)),
                   pl.BlockSpec((N_HEADS, D_QK, tm), lambda i: (0, 0, i))],
        out_shape=[
            jax.ShapeDtypeStruct((t, kvl), F32),
            jax.ShapeDtypeStruct((t, D_ROPE), F32),
            jax.ShapeDtypeStruct((N_HEADS, t, D_QK), BF16),
            jax.ShapeDtypeStruct((N_HEADS, D_QK, t), BF16),
        ],
        compiler_params=pltpu.CompilerParams(dimension_semantics=_arb(1), vmem_limit_bytes=48 << 20),
        name="latent_q_proj",
    )(x, cos, sin, cos.T, sin.T, *consts)


def _kv_up_kernel(ckv_ref, kpe_ref, wuk_ref, wuv_ref, k_ref, v_ref):
    c = ckv_ref[...].astype(BF16)
    kn = _dot(c, wuk_ref[...])
    vv = _dot(c, wuv_ref[...])
    kp = kpe_ref[...].astype(BF16)
    for h in range(N_HEADS):
        k_ref[h, :, 0:D_NOPE] = kn[:, h * D_NOPE:(h + 1) * D_NOPE].astype(BF16)
        k_ref[h, :, D_NOPE:D_QK] = kp
        v_ref[h] = vv[:, h * D_V:(h + 1) * D_V].astype(BF16)


def _kv_up(ckv, kpe, s, wuk2, wuv2):
    tm = ROW_TILE
    kvl = ckv.shape[1]
    return pl.pallas_call(
        _kv_up_kernel,
        grid=(s // tm,),
        in_specs=[
            pl.BlockSpec((tm, kvl), lambda i: (i, 0)),
            pl.BlockSpec((tm, D_ROPE), lambda i: (i, 0)),
            pl.BlockSpec(wuk2.shape, lambda i: (0, 0)),
            pl.BlockSpec(wuv2.shape, lambda i: (0, 0)),
        ],
        out_specs=[
            pl.BlockSpec((N_HEADS, tm, D_QK), lambda i: (0, i, 0)),
            pl.BlockSpec((N_HEADS, tm, D_V), lambda i: (0, i, 0)),
        ],
        out_shape=[
            jax.ShapeDtypeStruct((N_HEADS, s, D_QK), BF16),
            jax.ShapeDtypeStruct((N_HEADS, s, D_V), BF16),
        ],
        compiler_params=pltpu.CompilerParams(dimension_semantics=_arb(1)),
        name="kv_up",
    )(ckv, kpe, wuk2, wuv2)


def _flash_kernel(ti_ref, tj_ref, qt_ref, k_ref, v_ref, o_ref, m_s, l_s, acc_s):
    n = pl.program_id(1)
    i = ti_ref[n]
    j = tj_ref[n]
    ng, _, tq = qt_ref.shape
    tk = k_ref.shape[1]

    @pl.when(j == 0)
    def _():
        m_s[...] = jnp.full(m_s.shape, -jnp.inf, F32)
        l_s[...] = jnp.zeros(l_s.shape, F32)
        acc_s[...] = jnp.zeros(acc_s.shape, F32)

    def chain(u, masked):
        st = _dot(k_ref[u], qt_ref[u])
        if masked:
            key = lax.broadcasted_iota(I32, (tk, tq), 0)
            qry = lax.broadcasted_iota(I32, (tk, tq), 1)
            st = jnp.where(key <= qry, st, -jnp.inf)
        m_prev = m_s[u]
        m_new = jnp.maximum(m_prev, jnp.max(st, axis=0, keepdims=True))
        corr = jnp.exp2(m_prev - m_new)
        pt = jnp.exp2(st - m_new)
        l_s[u] = l_s[u] * corr + jnp.sum(pt, axis=0, keepdims=True)
        acc_s[u] = acc_s[u] * corr + _dot_tn(v_ref[u], pt.astype(BF16))
        m_s[u] = m_new

    @pl.when(j < i)
    def _():
        for u in range(ng):
            chain(u, False)

    @pl.when(j == i)
    def _():
        for u in range(ng):
            chain(u, True)
            o_ref[:, u * D_V:(u + 1) * D_V] = (acc_s[u] / l_s[u]).T.astype(o_ref.dtype)


def _flash_prompt(q_t, k_cat, v, s):
    tq = ATT_TQ
    ng = ATT_HEAD_GROUP
    assert tq == ATT_TK and s % tq == 0 and N_HEADS % ng == 0
    nq = s // tq
    pairs = [(i, j) for i in range(nq) for j in range(i + 1)]
    ti = jnp.asarray([p[0] for p in pairs], I32)
    tj = jnp.asarray([p[1] for p in pairs], I32)
    return pl.pallas_call(
        _flash_kernel,
        grid_spec=pltpu.PrefetchScalarGridSpec(
            num_scalar_prefetch=2,
            grid=(N_HEADS // ng, len(pairs)),
            in_specs=[
                pl.BlockSpec((ng, D_QK, tq), lambda g, n, ti, tj: (g, 0, ti[n])),
                pl.BlockSpec((ng, tq, D_QK), lambda g, n, ti, tj: (g, tj[n], 0)),
                pl.BlockSpec((ng, tq, D_V), lambda g, n, ti, tj: (g, tj[n], 0)),
            ],
            out_specs=pl.BlockSpec((tq, ng * D_V), lambda g, n, ti, tj: (ti[n], g)),
            scratch_shapes=[
                pltpu.VMEM((ng, 1, tq), F32), pltpu.VMEM((ng, 1, tq), F32),
                pltpu.VMEM((ng, D_V, tq), F32),
            ],
        ),
        out_shape=jax.ShapeDtypeStruct((s, N_HEADS * D_V), BF16),
        compiler_params=pltpu.CompilerParams(dimension_semantics=_arb(2), vmem_limit_bytes=48 << 20),
        name="flash_prompt",
    )(ti, tj, q_t, k_cat, v)


def _q_absorb_kernel(q_ref, wukt_ref, qlat_ref, qpe_ref):
    q = q_ref[...]
    qlat_ref[...] = _dot(q[:, 0:D_NOPE], wukt_ref[...])
    qpe_ref[...] = q[:, D_NOPE:D_QK].astype(F32)


def _q_absorb(q_cat, row0, ts, wukt):
    kvl = wukt.shape[2]
    assert row0 % ts == 0
    rb = row0 // ts
    return pl.pallas_call(
        _q_absorb_kernel,
        grid=(N_HEADS,),
        in_specs=[
            pl.BlockSpec((None, ts, D_QK), lambda h: (h, rb, 0)),
            pl.BlockSpec((None, D_NOPE, kvl), lambda h: (h, 0, 0)),
        ],
        out_specs=[
            pl.BlockSpec((None, ts, kvl), lambda h: (h, 0, 0)),
            pl.BlockSpec((None, ts, D_ROPE), lambda h: (h, 0, 0)),
        ],
        out_shape=[
            jax.ShapeDtypeStruct((N_HEADS, ts, kvl), F32),
            jax.ShapeDtypeStruct((N_HEADS, ts, D_ROPE), F32),
        ],
        compiler_params=pltpu.CompilerParams(dimension_semantics=_arb(1)),
        name="q_absorb",
    )(q_cat, wukt)


def _dec_attn_kernel(n_chunks, pt_ref, qlat_ref, qpe_ref, cnew_ref, pnew_ref, ckv_hbm, kpe_hbm, o_ref,
                     cbuf, pbuf, sem, m_s, l_s, acc_s, qlt_s, qpt_s, cn_s, pn_s, st_s):
    b = pl.program_id(0)
    c = pl.program_id(1)
    nb = pl.num_programs(0)
    n = b * n_chunks + c
    slot = lax.rem(n, 2)
    npg = DEC_PAGES
    nh, sq, kvl = qlat_ref.shape
    rows = nh * sq

    def fetch(step, slot_):
        bb = step // n_chunks
        cc = step - bb * n_chunks

        def body(i, carry):
            page = pt_ref[bb, cc * npg + i]
            pltpu.make_async_copy(ckv_hbm.at[page], cbuf.at[slot_, i], sem.at[0, slot_]).start()
            pltpu.make_async_copy(kpe_hbm.at[page], pbuf.at[slot_, i], sem.at[1, slot_]).start()
            return carry
        lax.fori_loop(0, npg, body, 0, unroll=8)

    @pl.when(n == 0)
    def _():
        fetch(n, slot)

    @pl.when(n + 1 < nb * n_chunks)
    def _():
        fetch(n + 1, 1 - slot)

    def to_col(v):
        eye = (lax.broadcasted_iota(I32, (rows, rows), 0) == lax.broadcasted_iota(I32, (rows, rows), 1))
        return jnp.sum(jnp.where(eye, jnp.broadcast_to(v, (rows, rows)), 0.0), axis=1, keepdims=True)

    @pl.when(c == 0)
    def _():
        qlt_s[...] = qlat_ref[...].reshape(rows, kvl).T.astype(BF16)
        qpt_s[...] = qpe_ref[...].reshape(rows, D_ROPE).T.astype(BF16)
        cn_s[...] = jnp.zeros(cn_s.shape, BF16)
        pn_s[...] = jnp.zeros(pn_s.shape, BF16)
        cn_s[0:sq, :] = cnew_ref[...].astype(BF16)
        pn_s[0:sq, :] = pnew_ref[...].astype(BF16)
        cn = cn_s[...]
        st = _dot(cn, qlt_s[...]) + _dot(pn_s[...], qpt_s[...])
        tkey = lax.broadcasted_iota(I32, st.shape, 0)
        srow = lax.rem(lax.broadcasted_iota(I32, st.shape, 1), sq)
        st = jnp.where(tkey <= srow, st, -jnp.inf)
        m = jnp.max(st, axis=0, keepdims=True)
        pt = jnp.exp(st - m)
        m_s[...] = m
        l_s[...] = jnp.sum(pt, axis=0, keepdims=True)
        acc_s[...] = _dot_tn(pt.astype(BF16), cn)

    pltpu.make_async_copy(ckv_hbm.at[pl.ds(0, npg)], cbuf.at[slot], sem.at[0, slot]).wait()
    pltpu.make_async_copy(kpe_hbm.at[pl.ds(0, npg)], pbuf.at[slot], sem.at[1, slot]).wait()

    sub = DEC_SUB_PAGES
    nsub = npg // sub
    qlt = qlt_s[...]
    qpt = qpt_s[...]

    def sub_block(u):
        ck = cbuf[slot, u * sub:(u + 1) * sub].reshape(sub * PAGE_SIZE, kvl).astype(BF16)
        kpt = jnp.concatenate([pbuf[slot, u * sub + i] for i in range(sub)], axis=1).astype(BF16)
        return ck, kpt

    m_prev = m_s[...]
    m_new = m_prev
    for u in range(nsub):
        ck, kpt = sub_block(u)
        st = _dot(ck, qlt) + _dot_tn(kpt, qpt)
        st_s[u] = st
        m_new = jnp.maximum(m_new, jnp.max(st, axis=0, keepdims=True))
    l_step = jnp.zeros_like(m_new)
    acc_step = jnp.zeros(acc_s.shape, F32)
    for u in range(nsub):
        ck, _ = sub_block(u)
        pt = jnp.exp(st_s[u] - m_new)
        l_step = l_step + jnp.sum(pt, axis=0, keepdims=True)
        acc_step = acc_step + _dot_tn(pt.astype(BF16), ck)
    corr = jnp.exp(m_prev - m_new)
    l_new = l_s[...] * corr + l_step
    acc_new = acc_s[...] * to_col(corr) + acc_step
    m_s[...] = m_new
    l_s[...] = l_new
    acc_s[...] = acc_new

    @pl.when(c == n_chunks - 1)
    def _():
        o_ref[...] = (acc_new / to_col(l_new)).reshape(nh, sq, kvl)


def _dec_attn(q_lat, q_pe, ckv_new, kpe_new, cache_ckv, cache_kpe_t, page_table):
    nh, ts, kvl = q_lat.shape
    bsz, n_pages = page_table.shape
    sq = ts // bsz
    assert n_pages % DEC_PAGES == 0 and sq == V7X_SUBLANES
    n_chunks = n_pages // DEC_PAGES
    rows = nh * sq
    assert DEC_PAGES % DEC_SUB_PAGES == 0 and cache_kpe_t.shape[1:] == (D_ROPE, PAGE_SIZE)
    return pl.pallas_call(
        functools.partial(_dec_attn_kernel, n_chunks),
        grid_spec=pltpu.PrefetchScalarGridSpec(
            num_scalar_prefetch=1,
            grid=(bsz, n_chunks),
            in_specs=[
                pl.BlockSpec((nh, sq, kvl), lambda b, c, pt: (0, b, 0)),
                pl.BlockSpec((nh, sq, D_ROPE), lambda b, c, pt: (0, b, 0)),
                pl.BlockSpec((sq, kvl), lambda b, c, pt: (b, 0)),
                pl.BlockSpec((sq, D_ROPE), lambda b, c, pt: (b, 0)),
                pl.BlockSpec(memory_space=pl.ANY),
                pl.BlockSpec(memory_space=pl.ANY),
            ],
            out_specs=pl.BlockSpec((nh, sq, kvl), lambda b, c, pt: (0, b, 0)),
            scratch_shapes=[
                pltpu.VMEM((2, DEC_PAGES, PAGE_SIZE, kvl), F32),
                pltpu.VMEM((2, DEC_PAGES, D_ROPE, PAGE_SIZE), F32),
                pltpu.SemaphoreType.DMA((2, 2)),
                pltpu.VMEM((1, rows), F32), pltpu.VMEM((1, rows), F32), pltpu.VMEM((rows, kvl), F32),
                pltpu.VMEM((kvl, rows), BF16), pltpu.VMEM((D_ROPE, rows), BF16),
                pltpu.VMEM((PAGE_SIZE, kvl), BF16), pltpu.VMEM((PAGE_SIZE, D_ROPE), BF16),
                pltpu.VMEM((DEC_PAGES // DEC_SUB_PAGES, DEC_SUB_PAGES * PAGE_SIZE, rows), F32),
            ],
        ),
        out_shape=jax.ShapeDtypeStruct((nh, ts, kvl), F32),
        compiler_params=pltpu.CompilerParams(dimension_semantics=_arb(2), vmem_limit_bytes=V7X_VMEM_LIMIT),
        name="dec_attn",
    )(page_table, q_lat, q_pe, ckv_new, kpe_new, cache_ckv, cache_kpe_t)


def _v_up_kernel(ol_ref, wuv_ref, o_ref):
    o_ref[...] = _dot(ol_ref[...].astype(BF16), wuv_ref[...]).astype(o_ref.dtype)


def _v_up(o_lat, wuv_h):
    nh, ts, kvl = o_lat.shape
    return pl.pallas_call(
        _v_up_kernel,
        grid=(nh,),
        in_specs=[
            pl.BlockSpec((None, ts, kvl), lambda h: (h, 0, 0)),
            pl.BlockSpec((None, kvl, D_V), lambda h: (h, 0, 0)),
        ],
        out_specs=pl.BlockSpec((ts, D_V), lambda h: (0, h)),
        out_shape=jax.ShapeDtypeStruct((ts, nh * D_V), BF16),
        compiler_params=pltpu.CompilerParams(dimension_semantics=_arb(1)),
        name="v_up",
    )(o_lat, wuv_h)


def _oproj_kernel(o_ref, x_ref, w_ref, y_ref):
    y_ref[...] = x_ref[...] + _dot(o_ref[...], w_ref[...])


def _oproj(o, x, w_o_bf):
    t, d = x.shape
    tm = ROW_TILE
    return pl.pallas_call(
        _oproj_kernel,
        grid=(t // tm,),
        in_specs=[
            pl.BlockSpec((tm, o.shape[1]), lambda i: (i, 0)),
            pl.BlockSpec((tm, d), lambda i: (i, 0)),
            pl.BlockSpec(w_o_bf.shape, lambda i: (0, 0)),
        ],
        out_specs=pl.BlockSpec((tm, d), lambda i: (i, 0)),
        out_shape=jax.ShapeDtypeStruct((t, d), F32),
        compiler_params=pltpu.CompilerParams(dimension_semantics=_arb(1), vmem_limit_bytes=40 << 20),
        name="attn_out_proj",
    )(o, x, w_o_bf)


def _rot_half_cols(w):
    half = D_ROPE // 2
    return jnp.concatenate([-w[..., half:], w[..., :half]], axis=-1)


def _rope_tables(pos):
    half = D_ROPE // 2
    inv_freq = jnp.power(ROPE_BASE, -jnp.arange(half, dtype=F32) / half)
    ang = pos.astype(F32)[:, None] * inv_freq[None, :]
    cos = jnp.cos(ang)
    sin = jnp.sin(ang)
    return jnp.concatenate([cos, cos], axis=-1), jnp.concatenate([sin, sin], axis=-1)


def kernel(x_prompt, x_sample, state_pool, cache_ckv, cache_kpe, page_table, g_mix, g_ffn, w_pool, s_pool,
           g_kv, w_dkv, g_ckv, w_uk, w_uv, w_dq, g_q, w_uq, w_o, w_router, b_router, w_up, b_up,
           w_down, b_down, g_final):
    bp, s, d = x_prompt.shape
    bd, sq, _ = x_sample.shape
    assert bp == 1 and g_mix.shape[0] == 2 and state_pool.shape[0] == 1
    ts = bd * sq
    t = s + ts
    past_len = page_table.shape[1] * cache_ckv.shape[1]
    kvl = w_uk.shape[0]
    row2 = lambda v: v.reshape(1, -1)

    w_pool_bf = w_pool[0].astype(BF16)
    x1p, pool_p = _pool_prompt(x_prompt[0], row2(g_mix[0]), w_pool_bf, row2(s_pool[0]))
    x1s, pool_s = _pool_sample(x_sample, state_pool[0], row2(g_mix[0]), w_pool_bf, row2(s_pool[0]), past_len)
    x1 = jnp.concatenate([x1p, x1s.reshape(ts, d)], axis=0)

    x2 = _moe_layer(x1, 0, row2(g_ffn[0]), w_router[0], row2(b_router[0]), w_up, b_up[0], w_down, b_down[0],
                    row2(g_final), False)

    pos = jnp.concatenate([jnp.arange(s, dtype=I32),
                           jnp.tile(past_len + jnp.arange(sq, dtype=I32), bd)])
    cos, sin = _rope_tables(pos)
    w_dkv_c = w_dkv[:, :kvl].astype(BF16)
    w_dkv_p = w_dkv[:, kvl:]
    w_uq1 = w_uq[0]
    ql = w_uq1.shape[0]
    w_qn = w_uq1[:, :, :D_NOPE].reshape(ql, N_HEADS * D_NOPE).astype(BF16)
    w_qp = w_uq1[:, :, D_NOPE:]
    ckv, kpe, q_cat, q_t = _proj(
        x2, cos, sin, row2(g_kv), row2(g_mix[1]), row2(g_ckv), row2(g_q[0]),
        w_dkv_c, w_dkv_p.astype(BF16), _rot_half_cols(w_dkv_p).astype(BF16), w_dq[0].astype(BF16),
        w_qn, w_qp.reshape(ql, N_HEADS * D_ROPE).astype(BF16),
        _rot_half_cols(w_qp).reshape(ql, N_HEADS * D_ROPE).astype(BF16))

    wuk2 = w_uk.reshape(kvl, N_HEADS * D_NOPE).astype(BF16)
    wuv2 = w_uv.reshape(kvl, N_HEADS * D_V).astype(BF16)
    k_cat, v = _kv_up(ckv, kpe, s, wuk2, wuv2)
    o_p = _flash_prompt(q_t, k_cat, v, s)

    wukt = jnp.transpose(w_uk, (1, 2, 0)).astype(BF16)
    wuv_h = jnp.transpose(w_uv, (1, 0, 2)).astype(BF16)
    q_lat, q_pe = _q_absorb(q_cat, s, ts, wukt)
    ckv_s = ckv[s:]
    kpe_s = kpe[s:]
    o_lat = _dec_attn(q_lat, q_pe, ckv_s, kpe_s, cache_ckv, jnp.swapaxes(cache_kpe, 1, 2), page_table)
    o_s = _v_up(o_lat, wuv_h)

    o = jnp.concatenate([o_p, o_s], axis=0)
    x3 = _oproj(o, x2, w_o[0].astype(BF16))
    y = _moe_layer(x3, 1, row2(g_ffn[1]), w_router[1], row2(b_router[1]), w_up, b_up[1], w_down, b_down[1],
                   row2(g_final), True)

    return (y[:s].reshape(1, s, d), y[s:].reshape(bd, sq, d),
            pool_p.reshape(1, 1, POOL_BUF, d), pool_s.reshape(1, bd, POOL_BUF, d),
            ckv[:s].reshape(1, s, kvl), kpe[:s].reshape(1, s, D_ROPE),
            ckv_s.reshape(bd, sq, kvl), kpe_s.reshape(bd, sq, D_ROPE))
```
